```python
import math
import jax, jax.numpy as jnp
from jax import lax
import numpy as np

D_MODEL = 1024
BATCH = 16
SEQ = 2048
DEPTH = 4

BLOCK = 128
RET_HEADS = 4
RET_DIM = 128
SB_HEADS = 8
SB_DIM = 64
DIFF_HEADS = 8
DIFF_DIM = 64
REL_BUCKETS = 32
REL_MAX_DIST = 128
D_FF = 2816
CONV_W = 3
ALPHA = (2 * DEPTH) ** 0.25
BETA = (8 * DEPTH) ** -0.25
LN_EPS = 1e-5
ROPE_BASE = 10000.0

RET_W = RET_HEADS * RET_DIM
SB_W = SB_HEADS * SB_DIM
EVEN_IN = 4 * RET_W + 3 * SB_W
EVEN_SPLITS = [RET_W, 2 * RET_W, 3 * RET_W, 4 * RET_W, 4 * RET_W + SB_W, 4 * RET_W + 2 * SB_W]
DIFF_W = DIFF_HEADS * 2 * DIFF_DIM
ODD_IN = 3 * DIFF_W
N_EVEN = (DEPTH + 1) // 2
N_ODD = DEPTH // 2

kernel_name = "retention_stickbreak_diffattn_convffn_deepnorm"


def layer_norm(x, g, b):
    xf = x.astype(jnp.float32)
    mu = xf.mean(-1, keepdims=True)
    var = jnp.square(xf - mu).mean(-1, keepdims=True)
    return ((xf - mu) * lax.rsqrt(var + LN_EPS) * g.astype(jnp.float32) + b.astype(jnp.float32)).astype(x.dtype)


def rotary(x):
    S, dh = x.shape[1], x.shape[-1]
    inv = ROPE_BASE ** (-jnp.arange(0, dh, 2, dtype=jnp.float32) / dh)
    ang = jnp.arange(S, dtype=jnp.float32)[:, None] * inv[None, :]
    cos = jnp.cos(ang)[None, :, None, :]
    sin = jnp.sin(ang)[None, :, None, :]
    xf = x.astype(jnp.float32)
    x1, x2 = xf[..., : dh // 2], xf[..., dh // 2:]
    return jnp.concatenate([x1 * cos - x2 * sin, x1 * sin + x2 * cos], -1).astype(x.dtype)


def retention(q, k, v):
    bsz, S, H, dh = q.shape
    n = S // BLOCK
    log_g = jnp.log(1.0 - 2.0 ** (-5.0 - jnp.arange(H, dtype=jnp.float32)))
    idx = jnp.arange(BLOCK, dtype=jnp.float32)
    rel = idx[:, None] - idx[None, :]
    decay = jnp.where(rel >= 0, jnp.exp(log_g[:, None, None] * jnp.maximum(rel, 0.0)), 0.0)
    q_dec = jnp.exp(log_g[:, None] * (idx[None, :] + 1.0))
    k_dec = jnp.exp(log_g[:, None] * (BLOCK - 1.0 - idx[None, :]))
    chunk_g = jnp.exp(log_g * BLOCK)

    def chunks(t):
        return t.astype(jnp.float32).reshape(bsz, n, BLOCK, H, dh).transpose(1, 0, 3, 2, 4)

    qc, kc, vc = chunks(q), chunks(k) * (dh ** -0.5), chunks(v)

    def step(state, inp):
        qi, ki, vi = inp
        s = jnp.einsum('bhid,bhjd->bhij', qi, ki) * decay[None]
        intra = jnp.einsum('bhij,bhjd->bhid', s, vi)
        cross = jnp.einsum('bhid,bhde->bhie', qi, state) * q_dec[None, :, :, None]
        new_state = state * chunk_g[None, :, None, None] + jnp.einsum(
            'bhjd,bhje->bhde', ki * k_dec[None, :, :, None], vi)
        return new_state, intra + cross

    state0 = jnp.zeros((bsz, H, dh, dh), jnp.float32)
    _, out = lax.scan(step, state0, (qc, kc, vc))
    return out.transpose(1, 0, 3, 2, 4).reshape(bsz, S, H, dh)


def stick_breaking(q, k, v):
    bsz, S, H, dh = q.shape
    n = S // BLOCK
    qb = q.reshape(bsz, n, BLOCK, H, dh).transpose(1, 0, 3, 2, 4)
    kh = k.transpose(0, 2, 1, 3)
    vh = v.transpose(0, 2, 1, 3)
    kpos = jnp.arange(S)
    scale = dh ** -0.5

    def block(args):
        qi, start = args
        qpos = start + jnp.arange(BLOCK)
        z = jnp.einsum('bhid,bhsd->bhis', qi, kh).astype(jnp.float32) * scale
        past = (kpos[None, :] < qpos[:, None])[None, None]
        log_keep = jnp.where(past, jax.nn.log_sigmoid(-z), 0.0)
        between = lax.cumsum(log_keep, axis=3, reverse=True) - log_keep
        a = jnp.where(past, jnp.exp(jax.nn.log_sigmoid(z) + between), 0.0)
        return jnp.einsum('bhis,bhsd->bhid', a.astype(vh.dtype), vh)

    out = lax.map(block, (qb, jnp.arange(n) * BLOCK))
    return out.transpose(1, 0, 3, 2, 4).reshape(bsz, S, H * dh)


def t5_bucket(rel):
    n = jnp.maximum(rel, 0)
    max_exact = REL_BUCKETS // 2
    nf = jnp.maximum(n, 1).astype(jnp.float32)
    large = max_exact + (jnp.log(nf / max_exact) / math.log(REL_MAX_DIST / max_exact)
                         * (REL_BUCKETS - max_exact)).astype(jnp.int32)
    large = jnp.minimum(large, REL_BUCKETS - 1)
    return jnp.where(n < max_exact, n, large)


def diff_attention(q, k, v, lam, rel_bias):
    _, bsz, H, S, dh = q.shape
    n = S // BLOCK
    qb = q.reshape(2, bsz, H, n, BLOCK, dh).transpose(3, 0, 1, 2, 4, 5)
    kpos = jnp.arange(S)
    scale = dh ** -0.5
    table = rel_bias.astype(jnp.float32)

    def block(args):
        qi, start = args
        qpos = start + jnp.arange(BLOCK)
        rel = qpos[:, None] - kpos[None, :]
        bias = table[t5_bucket(rel)].transpose(2, 0, 1)
        s = jnp.einsum('pbhid,pbhsd->pbhis', qi, k).astype(jnp.float32) * scale + bias[None, None]
        s = jnp.where((rel >= 0)[None, None, None], s, -jnp.inf)
        p = jax.nn.softmax(s, axis=-1)
        w = p[0] - lam * p[1]
        return jnp.einsum('bhis,bhse->bhie', w.astype(v.dtype), v).astype(jnp.float32)

    out = lax.map(block, (qb, jnp.arange(n) * BLOCK))
    return out.transpose(1, 2, 0, 3, 4).reshape(bsz, H, S, 2 * dh)


def even_mixer(x, w_in, w_out):
    bsz, S, _ = x.shape
    h = x @ w_in
    rq, rk, rv, rg, sq, sk, sv = jnp.split(h, EVEN_SPLITS, axis=-1)
    hd = lambda t, nh, dh: t.reshape(bsz, S, nh, dh)
    ret = retention(rotary(hd(rq, RET_HEADS, RET_DIM)), rotary(hd(rk, RET_HEADS, RET_DIM)),
                    hd(rv, RET_HEADS, RET_DIM))
    mu = ret.mean(-1, keepdims=True)
    var = jnp.square(ret - mu).mean(-1, keepdims=True)
    ret = ((ret - mu) * lax.rsqrt(var + LN_EPS)).reshape(bsz, S, RET_W)
    ret = jax.nn.silu(rg.astype(jnp.float32)) * ret
    sb = stick_breaking(hd(sq, SB_HEADS, SB_DIM), hd(sk, SB_HEADS, SB_DIM), hd(sv, SB_HEADS, SB_DIM))
    y = jnp.concatenate([ret.astype(x.dtype), sb.astype(x.dtype)], axis=-1)
    return y @ w_out


def odd_mixer(x, w_in, w_out, lq1, lk1, lq2, lk2, sub_g, rel_bias, layer):
    bsz, S, _ = x.shape
    h = x @ w_in
    q, k, v = jnp.split(h, 3, axis=-1)
    q = q.reshape(bsz, S, DIFF_HEADS, 2, DIFF_DIM).transpose(3, 0, 2, 1, 4)
    k = k.reshape(bsz, S, DIFF_HEADS, 2, DIFF_DIM).transpose(3, 0, 2, 1, 4)
    v = v.reshape(bsz, S, DIFF_HEADS, 2 * DIFF_DIM).transpose(0, 2, 1, 3)
    lam_init = 0.8 - 0.6 * math.exp(-0.3 * layer)
    lam = (jnp.exp(jnp.sum(lq1.astype(jnp.float32) * lk1.astype(jnp.float32)))
           - jnp.exp(jnp.sum(lq2.astype(jnp.float32) * lk2.astype(jnp.float32))) + lam_init)
    o = diff_attention(q, k, v, lam, rel_bias)
    o = o * lax.rsqrt(jnp.square(o).mean(-1, keepdims=True) + LN_EPS) * sub_g.astype(jnp.float32)
    o = (o * (1.0 - lam_init)).transpose(0, 2, 1, 3).reshape(bsz, S, DIFF_W).astype(x.dtype)
    return o @ w_out


def conv_ffn(x, w_up, conv_w, conv_b, w_down):
    S = x.shape[1]
    h = x @ w_up
    hp = jnp.pad(h, ((0, 0), (CONV_W - 1, 0), (0, 0)))
    h = sum(hp[:, j:j + S] * conv_w[j] for j in range(CONV_W)) + conv_b
    u, g = jnp.split(h, 2, axis=-1)
    return (jax.nn.silu(g) * u) @ w_down


def setup_inputs(seed: int = 0) -> dict:
    key = jax.random.key(seed)
    ks = jax.random.split(key, 20)
    nrm = lambda k, shape, s: jax.random.normal(k, shape, jnp.float32) * s
    return {
        "x": nrm(ks[0], (BATCH, SEQ, D_MODEL), 1.0),
        "w_in_even": nrm(ks[1], (N_EVEN, D_MODEL, EVEN_IN), D_MODEL ** -0.5),
        "w_out_even": nrm(ks[2], (N_EVEN, RET_W + SB_W, D_MODEL), (RET_W + SB_W) ** -0.5 * BETA),
        "w_in_odd": nrm(ks[3], (N_ODD, D_MODEL, ODD_IN), D_MODEL ** -0.5),
        "w_out_odd": nrm(ks[4], (N_ODD, DIFF_W, D_MODEL), DIFF_W ** -0.5 * BETA),
        "lam_q1": nrm(ks[5], (N_ODD, DIFF_DIM), 0.1),
        "lam_k1": nrm(ks[6], (N_ODD, DIFF_DIM), 0.1),
        "lam_q2": nrm(ks[7], (N_ODD, DIFF_DIM), 0.1),
        "lam_k2": nrm(ks[8], (N_ODD, DIFF_DIM), 0.1),
        "subln_g": 1.0 + nrm(ks[9], (N_ODD, 2 * DIFF_DIM), 0.02),
        "rel_bias": nrm(ks[10], (REL_BUCKETS, DIFF_HEADS), 0.5),
        "w_up": nrm(ks[11], (DEPTH, D_MODEL, 2 * D_FF), D_MODEL ** -0.5),
        "conv_w": nrm(ks[12], (DEPTH, CONV_W, 2 * D_FF), CONV_W ** -0.5),
        "conv_b": nrm(ks[13], (DEPTH, 2 * D_FF), 0.02),
        "w_down": nrm(ks[14], (DEPTH, D_FF, D_MODEL), D_FF ** -0.5 * BETA),
        "ln1_g": 1.0 + nrm(ks[15], (DEPTH, D_MODEL), 0.02),
        "ln1_b": nrm(ks[16], (DEPTH, D_MODEL), 0.02),
        "ln2_g": 1.0 + nrm(ks[17], (DEPTH, D_MODEL), 0.02),
        "ln2_b": nrm(ks[18], (DEPTH, D_MODEL), 0.02),
    }


def reference(x, w_in_even, w_out_even, w_in_odd, w_out_odd, lam_q1, lam_k1, lam_q2, lam_k2,
              subln_g, rel_bias, w_up, conv_w, conv_b, w_down, ln1_g, ln1_b, ln2_g, ln2_b):
    for l in range(DEPTH):
        i = l // 2
        if l % 2 == 0:
            m = even_mixer(x, w_in_even[i], w_out_even[i])
        else:
            m = odd_mixer(x, w_in_odd[i], w_out_odd[i], lam_q1[i], lam_k1[i], lam_q2[i], lam_k2[i],
                          subln_g[i], rel_bias, l)
        x = layer_norm(ALPHA * x + m, ln1_g[l], ln1_b[l])
        f = conv_ffn(x, w_up[l], conv_w[l], conv_b[l], w_down[l])
        x = layer_norm(ALPHA * x + f, ln2_g[l], ln2_b[l])
    return x
```

```python
import functools
import math

import numpy as np
import jax
import jax.numpy as jnp
from jax import lax
from jax.experimental import pallas as pl
from jax.experimental.pallas import tpu as pltpu

D_MODEL = 1024
DEPTH = 4
BLOCK = 128
RET_HEADS = 4
RET_DIM = 128
SB_HEADS = 8
SB_DIM = 64
DIFF_HEADS = 8
DIFF_DIM = 64
REL_BUCKETS = 32
REL_MAX_DIST = 128
D_FF = 2816
CONV_W = 3
ALPHA = (2 * DEPTH) ** 0.25
LN_EPS = 1e-5
ROPE_BASE = 10000.0

RET_W = RET_HEADS * RET_DIM
SB_W = SB_HEADS * SB_DIM
EVEN_IN = 4 * RET_W + 3 * SB_W
DIFF_W = DIFF_HEADS * 2 * DIFF_DIM
ODD_IN = 3 * DIFF_W

LANES = 128
SUBLANES = 8
MXU_N = 256
VMEM_LIMIT = 56 * 1024 * 1024
NEG_BIG = -1e30
ROW_TILE = 512

F32 = jnp.float32
BF16 = jnp.bfloat16


def _params(semantics):
    return pltpu.CompilerParams(dimension_semantics=semantics, vmem_limit_bytes=VMEM_LIMIT)


def _resident(shape):
    return pl.BlockSpec(shape, lambda *_: (0,) * len(shape), pipeline_mode=pl.Buffered(1))


def _dot(a, b):
    return jnp.dot(a, b, preferred_element_type=F32)


def _dot_nt(a, b):
    return lax.dot_general(a, b, (((1,), (1,)), ((), ())), preferred_element_type=F32)


def _dot_tn(a, b):
    return lax.dot_general(a, b, (((0,), (0,)), ((), ())), preferred_element_type=F32)


IN_CHUNK = 512


def _in_proj_kernel(x_ref, w_ref, cos_ref, sin_ref, o_ref, *, n_out, rotary_cols):
    x = x_ref[...]
    for c in range(n_out // IN_CHUNK):
        lo = c * IN_CHUNK
        acc = _dot(x, w_ref[:, lo:lo + IN_CHUNK])
        if lo < rotary_cols:
            cosv, sinv = cos_ref[...], sin_ref[...]
            for hh in range(IN_CHUNK // RET_DIM):
                blk = acc[:, hh * RET_DIM:(hh + 1) * RET_DIM]
                rot = blk * cosv + pltpu.roll(blk, RET_DIM // 2, axis=1) * sinv
                o_ref[:, lo + hh * RET_DIM:lo + (hh + 1) * RET_DIM] = rot.astype(o_ref.dtype)
        else:
            o_ref[:, lo:lo + IN_CHUNK] = acc.astype(o_ref.dtype)


def _in_proj(x_bf, w, cos_t, sin_t, seq, rotary_cols, tm=ROW_TILE):
    m, k = x_bf.shape
    n_out = w.shape[1]
    assert seq % tm == 0 and n_out % IN_CHUNK == 0
    s_tiles = seq // tm
    return pl.pallas_call(
        functools.partial(_in_proj_kernel, n_out=n_out, rotary_cols=rotary_cols),
        out_shape=jax.ShapeDtypeStruct((m, n_out), BF16),
        grid=(m // tm,),
        in_specs=[
            pl.BlockSpec((tm, k), lambda i: (i, 0)),
            _resident((k, n_out)),
            pl.BlockSpec((tm, RET_DIM), lambda i: (i % s_tiles, 0)),
            pl.BlockSpec((tm, RET_DIM), lambda i: (i % s_tiles, 0)),
        ],
        out_specs=pl.BlockSpec((tm, n_out), lambda i: (i, 0)),
        compiler_params=_params(("parallel",)),
        name="in_proj",
    )(x_bf, w, cos_t, sin_t)


def _retention_tables():
    hh = np.arange(RET_HEADS, dtype=np.float32)
    log_g = np.log(np.float32(1.0) - np.float32(2.0) ** (np.float32(-5.0) - hh)).astype(np.float32)
    idx = np.arange(BLOCK, dtype=np.float32)
    rel = idx[:, None] - idx[None, :]
    scale = np.float32(RET_DIM ** -0.5)
    decay = np.where(rel >= 0, np.exp(log_g[:, None, None] * np.maximum(rel, 0.0)), 0.0) * scale
    q_dec = np.exp(log_g[:, None] * (idx[None, :] + 1.0))
    k_dec = np.exp(log_g[:, None] * (BLOCK - 1.0 - idx[None, :])) * scale
    chunk_g = np.exp(log_g * BLOCK)
    bcast = lambda t: np.broadcast_to(t[:, :, None], (RET_HEADS, BLOCK, LANES))
    return (decay.astype(np.float32), np.ascontiguousarray(bcast(q_dec), np.float32),
            np.ascontiguousarray(bcast(k_dec), np.float32), [float(g) for g in chunk_g])


def _retention_kernel(q_ref, k_ref, v_ref, g_ref, decay_ref, qdec_ref, kdec_ref, o_ref, state_ref,
                      *, chunk_g):
    @pl.when(pl.program_id(1) == 0)
    def _():
        state_ref[...] = jnp.zeros_like(state_ref)

    for hh in range(RET_HEADS):
        cols = slice(hh * RET_DIM, (hh + 1) * RET_DIM)
        q, k, v = q_ref[0, :, cols], k_ref[0, :, cols], v_ref[0, :, cols]
        state = state_ref[hh]
        s = _dot_nt(q, k) * decay_ref[hh]
        out = _dot(s.astype(BF16), v) + _dot(q, state.astype(BF16)) * qdec_ref[hh]
        k_scaled = (k.astype(F32) * kdec_ref[hh]).astype(BF16)
        state_ref[hh] = state * chunk_g[hh] + _dot_tn(k_scaled, v)
        mu = jnp.mean(out, axis=-1, keepdims=True)
        cen = out - mu
        var = jnp.mean(cen * cen, axis=-1, keepdims=True)
        gate = g_ref[0, :, cols].astype(F32)
        o_ref[0, :, cols] = (cen * lax.rsqrt(var + LN_EPS) * (gate * jax.nn.sigmoid(gate))).astype(o_ref.dtype)


def _retention(h3):
    bsz, seq, _ = h3.shape
    decay, q_dec, k_dec, chunk_g = _retention_tables()
    col = lambda j: pl.BlockSpec((1, BLOCK, RET_W), lambda b, c, j=j: (b, c, j))
    return pl.pallas_call(
        functools.partial(_retention_kernel, chunk_g=chunk_g),
        out_shape=jax.ShapeDtypeStruct((bsz, seq, RET_W), BF16),
        grid=(bsz, seq // BLOCK),
        in_specs=[col(0), col(1), col(2), col(3),
                  _resident((RET_HEADS, BLOCK, BLOCK)), _resident((RET_HEADS, BLOCK, LANES)),
                  _resident((RET_HEADS, BLOCK, LANES))],
        out_specs=pl.BlockSpec((1, BLOCK, RET_W), lambda b, c: (b, c, 0)),
        scratch_shapes=[pltpu.VMEM((RET_HEADS, RET_DIM, RET_DIM), F32)],
        compiler_params=_params(("parallel", "arbitrary")),
        name="retention",
    )(h3, h3, h3, h3, jnp.asarray(decay), jnp.asarray(q_dec), jnp.asarray(k_dec))


SB_PAIR = LANES // SB_DIM


def _split_bf16(x):
    hi = x.astype(BF16)
    lo = (x - hi.astype(F32)).astype(BF16)
    return hi, lo


def _stickbreak_kernel(q_ref, k_ref, v_ref, tri_ref, o_ref):
    qi = pl.program_id(2)
    lane = lax.broadcasted_iota(jnp.int32, (BLOCK, LANES), 1)
    row = lax.broadcasted_iota(jnp.int32, (BLOCK, BLOCK), 0)
    col = lax.broadcasted_iota(jnp.int32, (BLOCK, BLOCK), 1)
    past = col < row
    q = q_ref[0] * (SB_DIM ** -0.5)
    zero = jnp.zeros_like(q)
    q_heads = [jnp.where((lane // SB_DIM) == p, q, zero) for p in range(SB_PAIR)]
    tri = tri_ref[...]

    def tile(j, carry, diagonal):
        start = pl.multiple_of(j * BLOCK, BLOCK)
        kt = k_ref[0, pl.ds(start, BLOCK), :]
        vt = v_ref[0, pl.ds(start, BLOCK), :]
        new = []
        for p in range(SB_PAIR):
            run, acc = carry[2 * p], carry[2 * p + 1]
            z = _dot_nt(q_heads[p], kt)
            log_keep = -(jnp.maximum(z, 0.0) + jnp.log1p(jnp.exp(-jnp.abs(z))))
            if diagonal:
                log_keep = jnp.where(past, log_keep, 0.0)
            hi, lo = _split_bf16(log_keep)
            between = _dot(hi, tri) + _dot(lo, tri) + run
            a = jnp.exp(z + log_keep + between)
            if diagonal:
                a = jnp.where(past, a, 0.0)
            acc = acc + _dot(a.astype(BF16), vt)
            run = run + jnp.sum(log_keep, axis=-1, keepdims=True)
            new += [run, acc]
        return tuple(new)

    init = []
    for p in range(SB_PAIR):
        init += [jnp.zeros((BLOCK, 1), F32), jnp.zeros((BLOCK, LANES), F32)]
    carry = tile(qi, tuple(init), True)
    carry = lax.fori_loop(0, qi, lambda t, c: tile(qi - 1 - t, c, False), carry)
    out = carry[1]
    for p in range(1, SB_PAIR):
        out = jnp.where((lane // SB_DIM) == p, carry[2 * p + 1], out)
    o_ref[0] = out.astype(o_ref.dtype)


def _stickbreak(h3):
    bsz, seq, _ = h3.shape
    n_pairs = SB_W // LANES
    base = 4 * RET_W // LANES
    idx = np.arange(BLOCK)
    tri = jnp.asarray((idx[:, None] > idx[None, :]).astype(np.float32), BF16)
    return pl.pallas_call(
        _stickbreak_kernel,
        out_shape=jax.ShapeDtypeStruct((bsz, seq, SB_W), BF16),
        grid=(bsz, n_pairs, seq // BLOCK),
        in_specs=[
            pl.BlockSpec((1, BLOCK, LANES), lambda b, p, i: (b, i, base + p)),
            pl.BlockSpec((1, seq, LANES), lambda b, p, i: (b, 0, base + n_pairs + p)),
            pl.BlockSpec((1, seq, LANES), lambda b, p, i: (b, 0, base + 2 * n_pairs + p)),
            _resident((BLOCK, BLOCK)),
        ],
        out_specs=pl.BlockSpec((1, BLOCK, LANES), lambda b, p, i: (b, i, p)),
        compiler_params=_params(("parallel", "parallel", "arbitrary")),
        name="stickbreak",
    )(h3, h3, h3, tri)


def _bucket_tiles():
    rel = np.arange(2 * BLOCK)
    max_exact = REL_BUCKETS // 2
    nf = np.maximum(rel, 1).astype(np.float32)
    large = max_exact + (np.log(nf / np.float32(max_exact)) / np.float32(math.log(REL_MAX_DIST / max_exact))
                         * np.float32(REL_BUCKETS - max_exact)).astype(np.int32)
    bucket = np.where(rel < max_exact, rel, np.minimum(large, REL_BUCKETS - 1)).astype(np.int32)
    assert bucket[BLOCK - 1:].min() == REL_BUCKETS - 1
    idx = np.arange(BLOCK)
    d = idx[:, None] - idx[None, :]
    return np.stack([bucket[np.maximum(d, 0)], bucket[d + BLOCK]]).astype(np.int32)


def _diffattn_kernel(tab_ref, q_ref, k_ref, v_ref, bidx_ref, lam_ref, subg_ref, o_ref,
                     bias_ref, vext_ref, acc_ref, *, lam_init):
    head, b, qi = pl.program_id(0), pl.program_id(1), pl.program_id(2)
    far = REL_BUCKETS - 1

    @pl.when((b == 0) & (qi == 0))
    def _():
        row = lax.broadcasted_iota(jnp.int32, (BLOCK, BLOCK), 0)
        col = lax.broadcasted_iota(jnp.int32, (BLOCK, BLOCK), 1)
        for t in range(2):
            idx = bidx_ref[t]
            bias = jnp.zeros((BLOCK, BLOCK), F32)
            for bk in range(REL_BUCKETS):
                bias = jnp.where(idx == bk, tab_ref[bk, head] - tab_ref[far, head], bias)
            if t == 0:
                bias = jnp.where(col <= row, bias, NEG_BIG)
            bias_ref[t] = bias
        bias_ref[2] = jnp.zeros((BLOCK, BLOCK), F32)

    @pl.when(qi == 0)
    def _():
        vext_ref[:, :LANES] = v_ref[0]
        vext_ref[:, LANES:] = jnp.ones((vext_ref.shape[0], LANES), vext_ref.dtype)

    lane = lax.broadcasted_iota(jnp.int32, (BLOCK, LANES), 1)
    q = q_ref[0] * (DIFF_DIM ** -0.5)
    zero = jnp.zeros_like(q)
    q_maps = [jnp.where((lane // DIFF_DIM) == p, q, zero) for p in range(2)]
    acc_ref[...] = jnp.zeros_like(acc_ref)

    def tile(j, m):
        start = pl.multiple_of(j * BLOCK, BLOCK)
        kt = k_ref[0, pl.ds(start, BLOCK), :]
        vt = vext_ref[pl.ds(start, BLOCK), :]
        bias = bias_ref[jnp.minimum(qi - j, 2)]
        new_m = []
        for p in range(2):
            s = _dot_nt(q_maps[p], kt) + bias
            m_new = jnp.maximum(m[p], jnp.max(s, axis=-1, keepdims=True))
            e = jnp.exp(s - m_new)
            acc_ref[p] = acc_ref[p] * jnp.exp(m[p] - m_new) + _dot(e.astype(BF16), vt)
            new_m.append(m_new)
        return tuple(new_m)

    m0 = jnp.full((BLOCK, 1), NEG_BIG, F32)
    lax.fori_loop(0, qi + 1, tile, (m0, m0))

    lam_v = lam_ref[...]
    lam = (jnp.exp(jnp.sum(lam_v[0:1] * lam_v[1:2], axis=-1, keepdims=True))
           - jnp.exp(jnp.sum(lam_v[2:3] * lam_v[3:4], axis=-1, keepdims=True)) + lam_init)
    o0 = acc_ref[0, :, :LANES] / acc_ref[0, :, LANES:]
    o1 = acc_ref[1, :, :LANES] / acc_ref[1, :, LANES:]
    o = o0 - lam * o1
    o = o * lax.rsqrt(jnp.mean(o * o, axis=-1, keepdims=True) + LN_EPS) * subg_ref[...]
    o_ref[0] = (o * (1.0 - lam_init)).astype(o_ref.dtype)


def _diffattn(h3, rel_bias, lam_params, sub_g, lam_init):
    bsz, seq, _ = h3.shape
    return pl.pallas_call(
        functools.partial(_diffattn_kernel, lam_init=lam_init),
        out_shape=jax.ShapeDtypeStruct((bsz, seq, DIFF_W), BF16),
        grid=(DIFF_HEADS, bsz, seq // BLOCK),
        in_specs=[
            pl.BlockSpec(memory_space=pltpu.SMEM),
            pl.BlockSpec((1, BLOCK, LANES), lambda h, b, i: (b, i, h)),
            pl.BlockSpec((1, seq, LANES), lambda h, b, i: (b, 0, DIFF_HEADS + h)),
            pl.BlockSpec((1, seq, LANES), lambda h, b, i: (b, 0, 2 * DIFF_HEADS + h)),
            _resident((2, BLOCK, BLOCK)),
            _resident((4, DIFF_DIM)),
            _resident((1, 2 * DIFF_DIM)),
        ],
        out_specs=pl.BlockSpec((1, BLOCK, LANES), lambda h, b, i: (b, i, h)),
        scratch_shapes=[pltpu.VMEM((3, BLOCK, BLOCK), F32),
                        pltpu.VMEM((seq, 2 * LANES), BF16),
                        pltpu.VMEM((2, BLOCK, 2 * LANES), F32)],
        compiler_params=_params(("arbitrary", "arbitrary", "arbitrary")),
        name="diffattn",
    )(rel_bias, h3, h3, h3, jnp.asarray(_bucket_tiles()), lam_params, sub_g)


def _proj_ln_kernel(*refs, n_in):
    a_refs, w_refs = refs[:n_in], refs[n_in:2 * n_in]
    x_ref, g_ref, b_ref, o_ref, obf_ref = refs[2 * n_in:]
    y = ALPHA * x_ref[...]
    for a_ref, w_ref in zip(a_refs, w_refs):
        y = y + _dot(a_ref[...], w_ref[...])
    mu = jnp.mean(y, axis=-1, keepdims=True)
    cen = y - mu
    var = jnp.mean(cen * cen, axis=-1, keepdims=True)
    out = cen * lax.rsqrt(var + LN_EPS) * g_ref[...] + b_ref[...]
    o_ref[...] = out
    obf_ref[...] = out.astype(obf_ref.dtype)


def _proj_ln(acts, weights, x, gain, bias, tm=ROW_TILE):
    m, d = x.shape
    assert m % tm == 0
    n_in = len(acts)
    in_specs = [pl.BlockSpec((tm, a.shape[1]), lambda i: (i, 0)) for a in acts]
    in_specs += [_resident(w.shape) for w in weights]
    in_specs += [pl.BlockSpec((tm, d), lambda i: (i, 0)), _resident((1, d)), _resident((1, d))]
    return pl.pallas_call(
        functools.partial(_proj_ln_kernel, n_in=n_in),
        out_shape=(jax.ShapeDtypeStruct((m, d), F32), jax.ShapeDtypeStruct((m, d), BF16)),
        grid=(m // tm,),
        in_specs=in_specs,
        out_specs=(pl.BlockSpec((tm, d), lambda i: (i, 0)), pl.BlockSpec((tm, d), lambda i: (i, 0))),
        compiler_params=_params(("parallel",)),
        name="proj_ln",
    )(*acts, *weights, x, gain.reshape(1, d), bias.reshape(1, d))


def _shift_rows(h, prev, n):
    row = lax.broadcasted_iota(jnp.int32, h.shape, 0)
    out = pltpu.roll(h, n, axis=0)
    for r in range(n):
        out = jnp.where(row == r, prev[SUBLANES - n + r:SUBLANES - n + r + 1], out)
    return out


def _ffn_up_kernel(x_ref, w_ref, cw_ref, cb_ref, o_ref, carry_ref, *, tiles_per_seq):
    x = x_ref[...]
    tm = x.shape[0]
    seq_start = (pl.program_id(0) % tiles_per_seq) == 0
    for c in range(D_FF // MXU_N):
        halves = []
        for base in (c * MXU_N, D_FF + c * MXU_N):
            cols = slice(base, base + MXU_N)
            h = _dot(x, w_ref[:, cols])
            prev = jnp.where(seq_start, 0.0, carry_ref[:, cols])
            carry_ref[:, cols] = h[tm - SUBLANES:]
            cw = cw_ref[:, cols]
            halves.append(_shift_rows(h, prev, 2) * cw[0:1] + _shift_rows(h, prev, 1) * cw[1:2]
                          + h * cw[2:3] + cb_ref[:, cols])
        u, g = halves
        o_ref[:, c * MXU_N:(c + 1) * MXU_N] = (g * jax.nn.sigmoid(g) * u).astype(o_ref.dtype)


def _ffn_up(x_bf, w_up, conv_w, conv_b, seq, tm=ROW_TILE):
    m, d = x_bf.shape
    assert seq % tm == 0
    return pl.pallas_call(
        functools.partial(_ffn_up_kernel, tiles_per_seq=seq // tm),
        out_shape=jax.ShapeDtypeStruct((m, D_FF), BF16),
        grid=(m // tm,),
        in_specs=[pl.BlockSpec((tm, d), lambda i: (i, 0)), _resident((d, 2 * D_FF)),
                  _resident((CONV_W, 2 * D_FF)), _resident((1, 2 * D_FF))],
        out_specs=pl.BlockSpec((tm, D_FF), lambda i: (i, 0)),
        scratch_shapes=[pltpu.VMEM((SUBLANES, 2 * D_FF), F32)],
        compiler_params=_params(("arbitrary",)),
        name="ffn_up",
    )(x_bf, w_up, conv_w, conv_b.reshape(1, 2 * D_FF))


def _rotary_tables(seq):
    inv = ROPE_BASE ** (-jnp.arange(0, RET_DIM, 2, dtype=F32) / RET_DIM)
    ang = jnp.arange(seq, dtype=F32)[:, None] * inv[None, :]
    cos, sin = jnp.cos(ang), jnp.sin(ang)
    return jnp.concatenate([cos, cos], -1), jnp.concatenate([-sin, sin], -1)


def kernel(x, w_in_even, w_out_even, w_in_odd, w_out_odd, lam_q1, lam_k1, lam_q2, lam_k2, subln_g,
           rel_bias, w_up, conv_w, conv_b, w_down, ln1_g, ln1_b, ln2_g, ln2_b):
    bsz, seq, d = x.shape
    m = bsz * seq
    cos_t, sin_t = _rotary_tables(seq)
    xf = x.reshape(m, d)
    xb = xf.astype(BF16)
    for l in range(DEPTH):
        i = l // 2
        if l % 2 == 0:
            h = _in_proj(xb, w_in_even[i].astype(BF16), cos_t, sin_t, seq, rotary_cols=2 * RET_W)
            h3 = h.reshape(bsz, seq, EVEN_IN)
            ret = _retention(h3).reshape(m, RET_W)
            sb = _stickbreak(h3).reshape(m, SB_W)
            w_o = w_out_even[i].astype(BF16)
            xf, xb = _proj_ln([ret, sb], [w_o[:RET_W], w_o[RET_W:]], xf, ln1_g[l], ln1_b[l])
        else:
            h = _in_proj(xb, w_in_odd[i].astype(BF16), cos_t, sin_t, seq, rotary_cols=0)
            lam_params = jnp.stack([lam_q1[i], lam_k1[i], lam_q2[i], lam_k2[i]]).astype(F32)
            lam_init = 0.8 - 0.6 * math.exp(-0.3 * l)
            o = _diffattn(h.reshape(bsz, seq, ODD_IN), rel_bias.astype(F32), lam_params,
                          subln_g[i].reshape(1, 2 * DIFF_DIM).astype(F32), lam_init)
            xf, xb = _proj_ln([o.reshape(m, DIFF_W)], [w_out_odd[i].astype(BF16)], xf, ln1_g[l], ln1_b[l])
        f = _ffn_up(xb, w_up[l].astype(BF16), conv_w[l], conv_b[l], seq)
        xf, xb = _proj_ln([f], [w_down[l].astype(BF16)], xf, ln2_g[l], ln2_b[l])
    return xf.reshape(bsz, seq, d)
```

```python
import functools
import math

import numpy as np
import jax
import jax.numpy as jnp
from jax import lax
from jax.experimental import pallas as pl
from jax.experimental.pallas import tpu as pltpu

D_MODEL = 1024
DEPTH = 4
BLOCK = 128
RET_HEADS = 4
RET_DIM = 128
SB_HEADS = 8
SB_DIM = 64
DIFF_HEADS = 8
DIFF_DIM = 64
REL_BUCKETS = 32
REL_MAX_DIST = 128
D_FF = 2816
CONV_W = 3
ALPHA = (2 * DEPTH) ** 0.25
LN_EPS = 1e-5
ROPE_BASE = 10000.0

RET_W = RET_HEADS * RET_DIM
SB_W = SB_HEADS * SB_DIM
EVEN_IN = 4 * RET_W + 3 * SB_W
DIFF_W = DIFF_HEADS * 2 * DIFF_DIM
ODD_IN = 3 * DIFF_W

LANES = 128
SUBLANES = 8
MXU_N = 256
VMEM_LIMIT = 56 * 1024 * 1024
NEG_BIG = -1e30
ROW_TILE = 512

F32 = jnp.float32
BF16 = jnp.bfloat16


def _params(semantics):
    return pltpu.CompilerParams(dimension_semantics=semantics, vmem_limit_bytes=VMEM_LIMIT)


def _resident(shape):
    return pl.BlockSpec(shape, lambda *_: (0,) * len(shape), pipeline_mode=pl.Buffered(1))


def _dot(a, b):
    return jnp.dot(a, b, preferred_element_type=F32)


def _dot_nt(a, b):
    return lax.dot_general(a, b, (((1,), (1,)), ((), ())), preferred_element_type=F32)


def _dot_tn(a, b):
    return lax.dot_general(a, b, (((0,), (0,)), ((), ())), preferred_element_type=F32)


def _lane_tile(x, reps):
    return x if reps == 1 else jnp.concatenate([x] * reps, axis=1)


def _stack_heads(q, width):
    lane = lax.broadcasted_iota(jnp.int32, q.shape, 1)
    zero = jnp.zeros_like(q)
    return jnp.concatenate([jnp.where((lane // width) == p, q, zero) for p in range(LANES // width)], axis=0)


IN_CHUNK = 512


def _in_proj_kernel(x_ref, w_ref, cos_ref, sin_ref, o_ref, *, n_out, rotary_cols):
    x = x_ref[...]
    for c in range(n_out // IN_CHUNK):
        lo = c * IN_CHUNK
        acc = _dot(x, w_ref[:, lo:lo + IN_CHUNK])
        if lo < rotary_cols:
            cosv, sinv = cos_ref[...], sin_ref[...]
            for hh in range(IN_CHUNK // RET_DIM):
                blk = acc[:, hh * RET_DIM:(hh + 1) * RET_DIM]
                rot = blk * cosv + pltpu.roll(blk, RET_DIM // 2, axis=1) * sinv
                o_ref[:, lo + hh * RET_DIM:lo + (hh + 1) * RET_DIM] = rot.astype(o_ref.dtype)
        else:
            o_ref[:, lo:lo + IN_CHUNK] = acc.astype(o_ref.dtype)


def _in_proj(x_bf, w, cos_t, sin_t, seq, rotary_cols, tm=ROW_TILE):
    m, k = x_bf.shape
    n_out = w.shape[1]
    assert seq % tm == 0 and n_out % IN_CHUNK == 0
    s_tiles = seq // tm
    return pl.pallas_call(
        functools.partial(_in_proj_kernel, n_out=n_out, rotary_cols=rotary_cols),
        out_shape=jax.ShapeDtypeStruct((m, n_out), BF16),
        grid=(m // tm,),
        in_specs=[
            pl.BlockSpec((tm, k), lambda i: (i, 0)),
            _resident((k, n_out)),
            pl.BlockSpec((tm, RET_DIM), lambda i: (i % s_tiles, 0)),
            pl.BlockSpec((tm, RET_DIM), lambda i: (i % s_tiles, 0)),
        ],
        out_specs=pl.BlockSpec((tm, n_out), lambda i: (i, 0)),
        compiler_params=_params(("parallel",)),
        name="in_proj",
    )(x_bf, w, cos_t, sin_t)


def _retention_tables():
    hh = np.arange(RET_HEADS, dtype=np.float32)
    log_g = np.log(np.float32(1.0) - np.float32(2.0) ** (np.float32(-5.0) - hh)).astype(np.float32)
    idx = np.arange(BLOCK, dtype=np.float32)
    rel = idx[:, None] - idx[None, :]
    scale = np.float32(RET_DIM ** -0.5)
    decay = np.where(rel >= 0, np.exp(log_g[:, None, None] * np.maximum(rel, 0.0)), 0.0) * scale
    q_dec = np.exp(log_g[:, None] * (idx[None, :] + 1.0))
    k_dec = np.exp(log_g[:, None] * (BLOCK - 1.0 - idx[None, :])) * scale
    chunk_g = np.exp(log_g * BLOCK)
    bcast = lambda t: np.broadcast_to(t[:, :, None], (RET_HEADS, BLOCK, LANES))
    return (decay.astype(np.float32), np.ascontiguousarray(bcast(q_dec), np.float32),
            np.ascontiguousarray(bcast(k_dec), np.float32), [float(g) for g in chunk_g])


def _retention_kernel(q_ref, k_ref, v_ref, g_ref, decay_ref, qdec_ref, kdec_ref, o_ref, state_ref,
                      *, chunk_g):
    @pl.when(pl.program_id(1) == 0)
    def _():
        state_ref[...] = jnp.zeros_like(state_ref)

    for hh in range(RET_HEADS):
        cols = slice(hh * RET_DIM, (hh + 1) * RET_DIM)
        q, k, v = q_ref[0, :, cols], k_ref[0, :, cols], v_ref[0, :, cols]
        state = state_ref[hh]
        s = _dot_nt(q, k) * decay_ref[hh]
        out = _dot(s.astype(BF16), v) + _dot(q, state.astype(BF16)) * qdec_ref[hh]
        k_scaled = (k.astype(F32) * kdec_ref[hh]).astype(BF16)
        state_ref[hh] = state * chunk_g[hh] + _dot_tn(k_scaled, v)
        mu = jnp.mean(out, axis=-1, keepdims=True)
        cen = out - mu
        var = jnp.mean(cen * cen, axis=-1, keepdims=True)
        gate = g_ref[0, :, cols].astype(F32)
        o_ref[0, :, cols] = (cen * lax.rsqrt(var + LN_EPS) * (gate * jax.nn.sigmoid(gate))).astype(o_ref.dtype)


def _retention(h3):
    bsz, seq, _ = h3.shape
    decay, q_dec, k_dec, chunk_g = _retention_tables()
    col = lambda j: pl.BlockSpec((1, BLOCK, RET_W), lambda b, c, j=j: (b, c, j))
    return pl.pallas_call(
        functools.partial(_retention_kernel, chunk_g=chunk_g),
        out_shape=jax.ShapeDtypeStruct((bsz, seq, RET_W), BF16),
        grid=(bsz, seq // BLOCK),
        in_specs=[col(0), col(1), col(2), col(3),
                  _resident((RET_HEADS, BLOCK, BLOCK)), _resident((RET_HEADS, BLOCK, LANES)),
                  _resident((RET_HEADS, BLOCK, LANES))],
        out_specs=pl.BlockSpec((1, BLOCK, RET_W), lambda b, c: (b, c, 0)),
        scratch_shapes=[pltpu.VMEM((RET_HEADS, RET_DIM, RET_DIM), F32)],
        compiler_params=_params(("parallel", "arbitrary")),
        name="retention",
    )(h3, h3, h3, h3, jnp.asarray(decay), jnp.asarray(q_dec), jnp.asarray(k_dec))


SB_PAIR = LANES // SB_DIM
SB_KEYS = MXU_N
SB_DEAD = -104.0


def _split_bf16(x):
    hi = x.astype(BF16)
    lo = (x - hi.astype(F32)).astype(BF16)
    return hi, lo


def _stickbreak_kernel(q_ref, k_ref, v_ref, tri_ref, o_ref, acc_ref, run_ref):
    qi = pl.program_id(2)
    qs = _stack_heads(q_ref[0] * (SB_DIM ** -0.5), SB_DIM)
    tri = tri_ref[...]
    top = (qi * BLOCK) // SB_KEYS

    def block(jb, diagonal):
        start = pl.multiple_of(jb * SB_KEYS, SB_KEYS)
        kt = k_ref[0, pl.ds(start, SB_KEYS), :]
        vt = v_ref[0, pl.ds(start, SB_KEYS), :]
        z = _dot_nt(qs, kt)
        log_keep = -(jnp.maximum(z, 0.0) + jnp.log1p(jnp.exp(-jnp.abs(z))))
        if diagonal:
            r = lax.broadcasted_iota(jnp.int32, z.shape, 0)
            c = lax.broadcasted_iota(jnp.int32, z.shape, 1)
            past = c + start < (r % BLOCK) + qi * BLOCK
            log_keep = jnp.where(past, log_keep, 0.0)
        hi, lo = _split_bf16(log_keep)
        run = run_ref[...]
        between = _dot(hi, tri) + _dot(lo, tri) + _lane_tile(run, SB_KEYS // LANES)
        a = jnp.exp(z + log_keep + between)
        if diagonal:
            a = jnp.where(past, a, 0.0)
        acc_ref[...] += _dot(a.astype(BF16), vt)
        run = run + jnp.sum(log_keep, axis=-1, keepdims=True)
        run_ref[...] = run
        return jnp.max(run) > SB_DEAD

    acc_ref[...] = jnp.zeros_like(acc_ref)
    run_ref[...] = jnp.zeros_like(run_ref)
    alive = block(top, True)
    lax.while_loop(lambda c: (c[0] >= 0) & c[1], lambda c: (c[0] - 1, block(c[0], False)),
                   (top - 1, alive))
    lane = lax.broadcasted_iota(jnp.int32, (BLOCK, LANES), 1)
    out = acc_ref[:BLOCK]
    for p in range(1, SB_PAIR):
        out = jnp.where((lane // SB_DIM) == p, acc_ref[p * BLOCK:(p + 1) * BLOCK], out)
    o_ref[0] = out.astype(o_ref.dtype)


def _stickbreak(h3):
    bsz, seq, _ = h3.shape
    assert seq % SB_KEYS == 0
    n_pairs = SB_W // LANES
    base = 4 * RET_W // LANES
    idx = np.arange(SB_KEYS)
    tri = jnp.asarray((idx[:, None] > idx[None, :]).astype(np.float32), BF16)
    return pl.pallas_call(
        _stickbreak_kernel,
        out_shape=jax.ShapeDtypeStruct((bsz, seq, SB_W), BF16),
        grid=(bsz, n_pairs, seq // BLOCK),
        in_specs=[
            pl.BlockSpec((1, BLOCK, LANES), lambda b, p, i: (b, i, base + p)),
            pl.BlockSpec((1, seq, LANES), lambda b, p, i: (b, 0, base + n_pairs + p)),
            pl.BlockSpec((1, seq, LANES), lambda b, p, i: (b, 0, base + 2 * n_pairs + p)),
            _resident((SB_KEYS, SB_KEYS)),
        ],
        out_specs=pl.BlockSpec((1, BLOCK, LANES), lambda b, p, i: (b, i, p)),
        scratch_shapes=[pltpu.VMEM((SB_PAIR * BLOCK, LANES), F32),
                        pltpu.VMEM((SB_PAIR * BLOCK, LANES), F32)],
        compiler_params=_params(("parallel", "parallel", "arbitrary")),
        name="stickbreak",
    )(h3, h3, h3, tri)


DA_TQ = 256
DA_TK = 512
DA_QB = DA_TQ // BLOCK
DA_KB = DA_TK // BLOCK


def _bucket_tiles():
    rel = np.arange(2 * BLOCK)
    max_exact = REL_BUCKETS // 2
    nf = np.maximum(rel, 1).astype(np.float32)
    large = max_exact + (np.log(nf / np.float32(max_exact)) / np.float32(math.log(REL_MAX_DIST / max_exact))
                         * np.float32(REL_BUCKETS - max_exact)).astype(np.int32)
    bucket = np.where(rel < max_exact, rel, np.minimum(large, REL_BUCKETS - 1)).astype(np.int32)
    assert bucket[BLOCK - 1:].min() == REL_BUCKETS - 1
    idx = np.arange(BLOCK)
    d = idx[:, None] - idx[None, :]
    return np.stack([bucket[np.maximum(d, 0)], bucket[d + BLOCK]]).astype(np.int32)


def _diffattn_kernel(tab_ref, q_ref, k_ref, v_ref, bidx_ref, lam_ref, subg_ref, o_ref,
                     bias_ref, vext_ref, acc_ref, m_ref, *, lam_init):
    head, b, qi = pl.program_id(0), pl.program_id(1), pl.program_id(2)
    far = REL_BUCKETS - 1

    @pl.when((b == 0) & (qi == 0))
    def _():
        row = lax.broadcasted_iota(jnp.int32, (BLOCK, BLOCK), 0)
        col = lax.broadcasted_iota(jnp.int32, (BLOCK, BLOCK), 1)
        bias_ref[0] = jnp.full((BLOCK, BLOCK), NEG_BIG, F32)
        for t in range(2):
            idx = bidx_ref[t]
            bias = jnp.zeros((BLOCK, BLOCK), F32)
            for bk in range(REL_BUCKETS):
                bias = jnp.where(idx == bk, tab_ref[bk, head] - tab_ref[far, head], bias)
            if t == 0:
                bias = jnp.where(col <= row, bias, NEG_BIG)
            bias_ref[1 + t] = bias
        bias_ref[3] = jnp.zeros((BLOCK, BLOCK), F32)

    @pl.when(qi == 0)
    def _():
        vext_ref[:, :LANES] = v_ref[0]
        vext_ref[:, LANES:] = jnp.ones((vext_ref.shape[0], LANES), vext_ref.dtype)

    qs = _stack_heads(q_ref[0] * (DIFF_DIM ** -0.5), DIFF_DIM)
    acc_ref[...] = jnp.zeros_like(acc_ref)
    m_ref[...] = jnp.full(m_ref.shape, NEG_BIG, F32)

    def step(j, carry, near):
        start = pl.multiple_of(j * DA_TK, DA_TK)
        s = _dot_nt(qs, k_ref[0, pl.ds(start, DA_TK), :])
        if near:
            rows = []
            for rb in range(DA_QB):
                dist = (qi * DA_QB + rb) - j * DA_KB
                rows.append(jnp.concatenate(
                    [bias_ref[jnp.clip(dist - cb, -1, 2) + 1] for cb in range(DA_KB)], axis=1))
            bias = jnp.concatenate(rows, axis=0)
            s = s + jnp.concatenate([bias, bias], axis=0)
        m_old = m_ref[...]
        m_new = jnp.maximum(m_old, jnp.max(s, axis=-1, keepdims=True))
        m_ref[...] = m_new
        p = jnp.exp(s - _lane_tile(m_new, DA_KB))
        acc_ref[...] = (acc_ref[...] * _lane_tile(jnp.exp(m_old - m_new), 2)
                        + _dot(p.astype(BF16), vext_ref[pl.ds(start, DA_TK), :]))
        return carry

    n_tiles = (qi * DA_TQ + DA_TQ - 1) // DA_TK + 1
    n_far = jnp.maximum((qi * DA_QB - 1) // DA_KB, 0)
    lax.fori_loop(0, n_far, functools.partial(step, near=False), 0)
    lax.fori_loop(n_far, n_tiles, functools.partial(step, near=True), 0)

    lam_v = lam_ref[...]
    lam = (jnp.exp(jnp.sum(lam_v[0:1] * lam_v[1:2], axis=-1, keepdims=True))
           - jnp.exp(jnp.sum(lam_v[2:3] * lam_v[3:4], axis=-1, keepdims=True)) + lam_init)
    o0 = acc_ref[:DA_TQ, :LANES] / acc_ref[:DA_TQ, LANES:]
    o1 = acc_ref[DA_TQ:, :LANES] / acc_ref[DA_TQ:, LANES:]
    o = o0 - lam * o1
    o = o * lax.rsqrt(jnp.mean(o * o, axis=-1, keepdims=True) + LN_EPS) * subg_ref[...]
    o_ref[0] = (o * (1.0 - lam_init)).astype(o_ref.dtype)


def _diffattn(h3, rel_bias, lam_params, sub_g, lam_init):
    bsz, seq, _ = h3.shape
    assert seq % DA_TK == 0 and DA_TK % DA_TQ == 0
    return pl.pallas_call(
        functools.partial(_diffattn_kernel, lam_init=lam_init),
        out_shape=jax.ShapeDtypeStruct((bsz, seq, DIFF_W), BF16),
        grid=(DIFF_HEADS, bsz, seq // DA_TQ),
        in_specs=[
            pl.BlockSpec(memory_space=pltpu.SMEM),
            pl.BlockSpec((1, DA_TQ, LANES), lambda h, b, i: (b, i, h)),
            pl.BlockSpec((1, seq, LANES), lambda h, b, i: (b, 0, DIFF_HEADS + h)),
            pl.BlockSpec((1, seq, LANES), lambda h, b, i: (b, 0, 2 * DIFF_HEADS + h)),
            _resident((2, BLOCK, BLOCK)),
            _resident((4, DIFF_DIM)),
            _resident((1, 2 * DIFF_DIM)),
        ],
        out_specs=pl.BlockSpec((1, DA_TQ, LANES), lambda h, b, i: (b, i, h)),
        scratch_shapes=[pltpu.VMEM((4, BLOCK, BLOCK), F32),
                        pltpu.VMEM((seq, 2 * LANES), BF16),
                        pltpu.VMEM((2 * DA_TQ, 2 * LANES), F32),
                        pltpu.VMEM((2 * DA_TQ, LANES), F32)],
        compiler_params=_params(("arbitrary", "arbitrary", "arbitrary")),
        name="diffattn",
    )(rel_bias, h3, h3, h3, jnp.asarray(_bucket_tiles()), lam_params, sub_g)


def _proj_ln_kernel(*refs, n_in):
    a_refs, w_refs = refs[:n_in], refs[n_in:2 * n_in]
    x_ref, g_ref, b_ref, o_ref, obf_ref = refs[2 * n_in:]
    y = ALPHA * x_ref[...]
    for a_ref, w_ref in zip(a_refs, w_refs):
        y = y + _dot(a_ref[...], w_ref[...])
    mu = jnp.mean(y, axis=-1, keepdims=True)
    cen = y - mu
    var = jnp.mean(cen * cen, axis=-1, keepdims=True)
    out = cen * lax.rsqrt(var + LN_EPS) * g_ref[...] + b_ref[...]
    o_ref[...] = out
    obf_ref[...] = out.astype(obf_ref.dtype)


def _proj_ln(acts, weights, x, gain, bias, tm=ROW_TILE):
    m, d = x.shape
    assert m % tm == 0
    n_in = len(acts)
    in_specs = [pl.BlockSpec((tm, a.shape[1]), lambda i: (i, 0)) for a in acts]
    in_specs += [_resident(w.shape) for w in weights]
    in_specs += [pl.BlockSpec((tm, d), lambda i: (i, 0)), _resident((1, d)), _resident((1, d))]
    return pl.pallas_call(
        functools.partial(_proj_ln_kernel, n_in=n_in),
        out_shape=(jax.ShapeDtypeStruct((m, d), F32), jax.ShapeDtypeStruct((m, d), BF16)),
        grid=(m // tm,),
        in_specs=in_specs,
        out_specs=(pl.BlockSpec((tm, d), lambda i: (i, 0)), pl.BlockSpec((tm, d), lambda i: (i, 0))),
        compiler_params=_params(("parallel",)),
        name="proj_ln",
    )(*acts, *weights, x, gain.reshape(1, d), bias.reshape(1, d))


def _shift_rows(h, prev, n):
    row = lax.broadcasted_iota(jnp.int32, h.shape, 0)
    out = pltpu.roll(h, n, axis=0)
    for r in range(n):
        out = jnp.where(row == r, prev[SUBLANES - n + r:SUBLANES - n + r + 1], out)
    return out


def _ffn_up_kernel(x_ref, w_ref, cw_ref, cb_ref, o_ref, carry_ref, *, tiles_per_seq):
    x = x_ref[...]
    tm = x.shape[0]
    seq_start = (pl.program_id(0) % tiles_per_seq) == 0
    for c in range(D_FF // MXU_N):
        halves = []
        for base in (c * MXU_N, D_FF + c * MXU_N):
            cols = slice(base, base + MXU_N)
            h = _dot(x, w_ref[:, cols])
            prev = jnp.where(seq_start, 0.0, carry_ref[:, cols])
            carry_ref[:, cols] = h[tm - SUBLANES:]
            cw = cw_ref[:, cols]
            halves.append(_shift_rows(h, prev, 2) * cw[0:1] + _shift_rows(h, prev, 1) * cw[1:2]
                          + h * cw[2:3] + cb_ref[:, cols])
        u, g = halves
        o_ref[:, c * MXU_N:(c + 1) * MXU_N] = (g * jax.nn.sigmoid(g) * u).astype(o_ref.dtype)


def _ffn_up(x_bf, w_up, conv_w, conv_b, seq, tm=ROW_TILE):
    m, d = x_bf.shape
    assert seq % tm == 0
    return pl.pallas_call(
        functools.partial(_ffn_up_kernel, tiles_per_seq=seq // tm),
        out_shape=jax.ShapeDtypeStruct((m, D_FF), BF16),
        grid=(m // tm,),
        in_specs=[pl.BlockSpec((tm, d), lambda i: (i, 0)), _resident((d, 2 * D_FF)),
                  _resident((CONV_W, 2 * D_FF)), _resident((1, 2 * D_FF))],
        out_specs=pl.BlockSpec((tm, D_FF), lambda i: (i, 0)),
        scratch_shapes=[pltpu.VMEM((SUBLANES, 2 * D_FF), F32)],
        compiler_params=_params(("arbitrary",)),
        name="ffn_up",
    )(x_bf, w_up, conv_w, conv_b.reshape(1, 2 * D_FF))


def _rotary_tables(seq):
    inv = ROPE_BASE ** (-jnp.arange(0, RET_DIM, 2, dtype=F32) / RET_DIM)
    ang = jnp.arange(seq, dtype=F32)[:, None] * inv[None, :]
    cos, sin = jnp.cos(ang), jnp.sin(ang)
    return jnp.concatenate([cos, cos], -1), jnp.concatenate([-sin, sin], -1)


def kernel(x, w_in_even, w_out_even, w_in_odd, w_out_odd, lam_q1, lam_k1, lam_q2, lam_k2, subln_g,
           rel_bias, w_up, conv_w, conv_b, w_down, ln1_g, ln1_b, ln2_g, ln2_b):
    bsz, seq, d = x.shape
    m = bsz * seq
    cos_t, sin_t = _rotary_tables(seq)
    xf = x.reshape(m, d)
    xb = xf.astype(BF16)
    for l in range(DEPTH):
        i = l // 2
        if l % 2 == 0:
            h = _in_proj(xb, w_in_even[i].astype(BF16), cos_t, sin_t, seq, rotary_cols=2 * RET_W)
            h3 = h.reshape(bsz, seq, EVEN_IN)
            ret = _retention(h3).reshape(m, RET_W)
            sb = _stickbreak(h3).reshape(m, SB_W)
            w_o = w_out_even[i].astype(BF16)
            xf, xb = _proj_ln([ret, sb], [w_o[:RET_W], w_o[RET_W:]], xf, ln1_g[l], ln1_b[l])
        else:
            h = _in_proj(xb, w_in_odd[i].astype(BF16), cos_t, sin_t, seq, rotary_cols=0)
            lam_params = jnp.stack([lam_q1[i], lam_k1[i], lam_q2[i], lam_k2[i]]).astype(F32)
            lam_init = 0.8 - 0.6 * math.exp(-0.3 * l)
            o = _diffattn(h.reshape(bsz, seq, ODD_IN), rel_bias.astype(F32), lam_params,
                          subln_g[i].reshape(1, 2 * DIFF_DIM).astype(F32), lam_init)
            xf, xb = _proj_ln([o.reshape(m, DIFF_W)], [w_out_odd[i].astype(BF16)], xf, ln1_g[l], ln1_b[l])
        f = _ffn_up(xb, w_up[l].astype(BF16), conv_w[l], conv_b[l], seq)
        xf, xb = _proj_ln([f], [w_down[l].astype(BF16)], xf, ln2_g[l], ln2_b[l])
    return xf.reshape(bsz, seq, d)
```

```python
import functools
import math

import numpy as np
import jax
import jax.numpy as jnp
from jax import lax
from jax.experimental import pallas as pl
from jax.experimental.pallas import tpu as pltpu

D_MODEL = 1024
DEPTH = 4
BLOCK = 128
RET_HEADS = 4
RET_DIM = 128
SB_HEADS = 8
SB_DIM = 64
DIFF_HEADS = 8
DIFF_DIM = 64
REL_BUCKETS = 32
REL_MAX_DIST = 128
D_FF = 2816
CONV_W = 3
ALPHA = (2 * DEPTH) ** 0.25
LN_EPS = 1e-5
ROPE_BASE = 10000.0

RET_W = RET_HEADS * RET_DIM
SB_W = SB_HEADS * SB_DIM
EVEN_IN = 4 * RET_W + 3 * SB_W
DIFF_W = DIFF_HEADS * 2 * DIFF_DIM
ODD_IN = 3 * DIFF_W

LANES = 128
SUBLANES = 8
MXU_N = 256
VMEM_LIMIT = 56 * 1024 * 1024
NEG_BIG = -1e30
ROW_TILE = 512

F32 = jnp.float32
BF16 = jnp.bfloat16


def _params(semantics):
    return pltpu.CompilerParams(dimension_semantics=semantics, vmem_limit_bytes=VMEM_LIMIT)


def _resident(shape):
    return pl.BlockSpec(shape, lambda *_: (0,) * len(shape), pipeline_mode=pl.Buffered(1))


def _dot(a, b):
    return jnp.dot(a, b, preferred_element_type=F32)


def _dot_nt(a, b):
    return lax.dot_general(a, b, (((1,), (1,)), ((), ())), preferred_element_type=F32)


def _dot_tn(a, b):
    return lax.dot_general(a, b, (((0,), (0,)), ((), ())), preferred_element_type=F32)


def _lane_tile(x, reps):
    return x if reps == 1 else jnp.concatenate([x] * reps, axis=1)


def _stack_heads(q, width):
    lane = lax.broadcasted_iota(jnp.int32, q.shape, 1)
    zero = jnp.zeros_like(q)
    return jnp.concatenate([jnp.where((lane // width) == p, q, zero) for p in range(LANES // width)], axis=0)


IN_CHUNK = 512


def _in_proj_kernel(x_ref, w_ref, cos_ref, sin_ref, o_ref, *, n_out, rotary_cols):
    x = x_ref[...]
    for c in range(n_out // IN_CHUNK):
        lo = c * IN_CHUNK
        acc = _dot(x, w_ref[:, lo:lo + IN_CHUNK])
        if lo < rotary_cols:
            cosv, sinv = cos_ref[...], sin_ref[...]
            for hh in range(IN_CHUNK // RET_DIM):
                blk = acc[:, hh * RET_DIM:(hh + 1) * RET_DIM]
                rot = blk * cosv + pltpu.roll(blk, RET_DIM // 2, axis=1) * sinv
                o_ref[:, lo + hh * RET_DIM:lo + (hh + 1) * RET_DIM] = rot.astype(o_ref.dtype)
        else:
            o_ref[:, lo:lo + IN_CHUNK] = acc.astype(o_ref.dtype)


def _in_proj(x_bf, w, cos_t, sin_t, seq, rotary_cols, tm=ROW_TILE):
    m, k = x_bf.shape
    n_out = w.shape[1]
    assert seq % tm == 0 and n_out % IN_CHUNK == 0
    s_tiles = seq // tm
    return pl.pallas_call(
        functools.partial(_in_proj_kernel, n_out=n_out, rotary_cols=rotary_cols),
        out_shape=jax.ShapeDtypeStruct((m, n_out), BF16),
        grid=(m // tm,),
        in_specs=[
            pl.BlockSpec((tm, k), lambda i: (i, 0)),
            _resident((k, n_out)),
            pl.BlockSpec((tm, RET_DIM), lambda i: (i % s_tiles, 0)),
            pl.BlockSpec((tm, RET_DIM), lambda i: (i % s_tiles, 0)),
        ],
        out_specs=pl.BlockSpec((tm, n_out), lambda i: (i, 0)),
        compiler_params=_params(("parallel",)),
        name="in_proj",
    )(x_bf, w, cos_t, sin_t)


def _retention_tables():
    hh = np.arange(RET_HEADS, dtype=np.float32)
    log_g = np.log(np.float32(1.0) - np.float32(2.0) ** (np.float32(-5.0) - hh)).astype(np.float32)
    idx = np.arange(BLOCK, dtype=np.float32)
    rel = idx[:, None] - idx[None, :]
    scale = np.float32(RET_DIM ** -0.5)
    decay = np.where(rel >= 0, np.exp(log_g[:, None, None] * np.maximum(rel, 0.0)), 0.0) * scale
    q_dec = np.exp(log_g[:, None] * (idx[None, :] + 1.0))
    k_dec = np.exp(log_g[:, None] * (BLOCK - 1.0 - idx[None, :])) * scale
    chunk_g = np.exp(log_g * BLOCK)
    bcast = lambda t: np.broadcast_to(t[:, :, None], (RET_HEADS, BLOCK, LANES))
    return (decay.astype(np.float32), np.ascontiguousarray(bcast(q_dec), np.float32),
            np.ascontiguousarray(bcast(k_dec), np.float32), [float(g) for g in chunk_g])


def _retention_kernel(q_ref, k_ref, v_ref, g_ref, decay_ref, qdec_ref, kdec_ref, o_ref, state_ref,
                      *, chunk_g):
    @pl.when(pl.program_id(1) == 0)
    def _():
        state_ref[...] = jnp.zeros_like(state_ref)

    for hh in range(RET_HEADS):
        cols = slice(hh * RET_DIM, (hh + 1) * RET_DIM)
        q, k, v = q_ref[0, :, cols], k_ref[0, :, cols], v_ref[0, :, cols]
        state = state_ref[hh]
        s = _dot_nt(q, k) * decay_ref[hh]
        out = _dot(s.astype(BF16), v) + _dot(q, state.astype(BF16)) * qdec_ref[hh]
        k_scaled = (k.astype(F32) * kdec_ref[hh]).astype(BF16)
        state_ref[hh] = state * chunk_g[hh] + _dot_tn(k_scaled, v)
        mu = jnp.mean(out, axis=-1, keepdims=True)
        cen = out - mu
        var = jnp.mean(cen * cen, axis=-1, keepdims=True)
        gate = g_ref[0, :, cols].astype(F32)
        o_ref[0, :, cols] = (cen * lax.rsqrt(var + LN_EPS) * (gate * jax.nn.sigmoid(gate))).astype(o_ref.dtype)


def _retention(h3):
    bsz, seq, _ = h3.shape
    decay, q_dec, k_dec, chunk_g = _retention_tables()
    col = lambda j: pl.BlockSpec((1, BLOCK, RET_W), lambda b, c, j=j: (b, c, j))
    return pl.pallas_call(
        functools.partial(_retention_kernel, chunk_g=chunk_g),
        out_shape=jax.ShapeDtypeStruct((bsz, seq, RET_W), BF16),
        grid=(bsz, seq // BLOCK),
        in_specs=[col(0), col(1), col(2), col(3),
                  _resident((RET_HEADS, BLOCK, BLOCK)), _resident((RET_HEADS, BLOCK, LANES)),
                  _resident((RET_HEADS, BLOCK, LANES))],
        out_specs=pl.BlockSpec((1, BLOCK, RET_W), lambda b, c: (b, c, 0)),
        scratch_shapes=[pltpu.VMEM((RET_HEADS, RET_DIM, RET_DIM), F32)],
        compiler_params=_params(("parallel", "arbitrary")),
        name="retention",
    )(h3, h3, h3, h3, jnp.asarray(decay), jnp.asarray(q_dec), jnp.asarray(k_dec))


SB_PAIR = LANES // SB_DIM
SB_KEYS = MXU_N
SB_DEAD = -104.0


def _split_bf16(x):
    hi = x.astype(BF16)
    lo = (x - hi.astype(F32)).astype(BF16)
    return hi, lo


def _stickbreak_kernel(q_ref, k_ref, v_ref, tri_ref, o_ref, acc_ref, run_ref):
    qi = pl.program_id(2)
    q = q_ref[0] * (SB_DIM ** -0.5)
    lane = lax.broadcasted_iota(jnp.int32, q.shape, 1)
    q_heads = [jnp.where((lane // SB_DIM) == p, q, jnp.zeros_like(q)) for p in range(SB_PAIR)]
    tri = tri_ref[...]

    def add_block(jb, runs, accs, diagonal=False, valid=None):
        start = pl.multiple_of(jb * SB_KEYS, SB_KEYS)
        kt = k_ref[0, pl.ds(start, SB_KEYS), :]
        vt = v_ref[0, pl.ds(start, SB_KEYS), :]
        if diagonal:
            r = lax.broadcasted_iota(jnp.int32, (SB_KEYS, SB_KEYS), 0)
            c = lax.broadcasted_iota(jnp.int32, (SB_KEYS, SB_KEYS), 1)
            past = c < r
        new_runs, new_accs = [], []
        for p in range(SB_PAIR):
            z = _dot_nt(q_heads[p], kt)
            log_take = jnp.minimum(z, 0.0) - jnp.log(1.0 + jnp.exp(-jnp.abs(z)))
            log_keep = log_take - z
            if diagonal:
                log_keep = jnp.where(past, log_keep, 0.0)
            if valid is not None:
                log_keep = jnp.where(valid, log_keep, 0.0)
            hi, lo = _split_bf16(log_keep)
            between = _dot(hi, tri) + _dot(lo, tri) + _lane_tile(runs[p], SB_KEYS // LANES)
            a = jnp.exp(log_take + between)
            if diagonal:
                a = jnp.where(past, a, 0.0)
            if valid is not None:
                a = jnp.where(valid, a, 0.0)
            new_accs.append(accs[p] + _dot(a.astype(BF16), vt))
            new_runs.append(runs[p] + jnp.sum(log_keep, axis=-1, keepdims=True))
        return new_runs, new_accs

    def any_alive(runs):
        return jnp.max(functools.reduce(jnp.maximum, runs)) > SB_DEAD

    zeros = [jnp.zeros((SB_KEYS, LANES), F32)] * SB_PAIR
    runs, accs = add_block(qi, zeros, zeros, diagonal=True)
    runs, accs = add_block(jnp.maximum(qi - 1, 0), runs, accs, valid=qi > 0)
    for p in range(SB_PAIR):
        run_ref[p], acc_ref[p] = runs[p], accs[p]

    def more(carry):
        jb, _ = carry
        runs, accs = add_block(jb, [run_ref[p] for p in range(SB_PAIR)],
                               [acc_ref[p] for p in range(SB_PAIR)])
        for p in range(SB_PAIR):
            run_ref[p], acc_ref[p] = runs[p], accs[p]
        return jb - 1, any_alive(runs)

    lax.while_loop(lambda c: (c[0] >= 0) & c[1], more, (qi - 2, any_alive(runs)))
    out = acc_ref[0]
    for p in range(1, SB_PAIR):
        out = jnp.where((lane // SB_DIM) == p, acc_ref[p], out)
    o_ref[0] = out.astype(o_ref.dtype)


def _stickbreak(h3):
    bsz, seq, _ = h3.shape
    assert seq % SB_KEYS == 0
    n_pairs = SB_W // LANES
    base = 4 * RET_W // LANES
    idx = np.arange(SB_KEYS)
    tri = jnp.asarray((idx[:, None] > idx[None, :]).astype(np.float32), BF16)
    return pl.pallas_call(
        _stickbreak_kernel,
        out_shape=jax.ShapeDtypeStruct((bsz, seq, SB_W), BF16),
        grid=(bsz, n_pairs, seq // SB_KEYS),
        in_specs=[
            pl.BlockSpec((1, SB_KEYS, LANES), lambda b, p, i: (b, i, base + p)),
            pl.BlockSpec((1, seq, LANES), lambda b, p, i: (b, 0, base + n_pairs + p)),
            pl.BlockSpec((1, seq, LANES), lambda b, p, i: (b, 0, base + 2 * n_pairs + p)),
            _resident((SB_KEYS, SB_KEYS)),
        ],
        out_specs=pl.BlockSpec((1, SB_KEYS, LANES), lambda b, p, i: (b, i, p)),
        scratch_shapes=[pltpu.VMEM((SB_PAIR, SB_KEYS, LANES), F32),
                        pltpu.VMEM((SB_PAIR, SB_KEYS, LANES), F32)],
        compiler_params=_params(("parallel", "parallel", "arbitrary")),
        name="stickbreak",
    )(h3, h3, h3, tri)


DA_TQ = 512
DA_TK = 512
DA_QB = DA_TQ // BLOCK
DA_KB = DA_TK // BLOCK


def _bucket_tiles():
    rel = np.arange(2 * BLOCK)
    max_exact = REL_BUCKETS // 2
    nf = np.maximum(rel, 1).astype(np.float32)
    large = max_exact + (np.log(nf / np.float32(max_exact)) / np.float32(math.log(REL_MAX_DIST / max_exact))
                         * np.float32(REL_BUCKETS - max_exact)).astype(np.int32)
    bucket = np.where(rel < max_exact, rel, np.minimum(large, REL_BUCKETS - 1)).astype(np.int32)
    assert bucket[BLOCK - 1:].min() == REL_BUCKETS - 1
    idx = np.arange(BLOCK)
    d = idx[:, None] - idx[None, :]
    return np.stack([bucket[np.maximum(d, 0)], bucket[d + BLOCK]]).astype(np.int32)


def _diffattn_kernel(tab_ref, q_ref, k_ref, v_ref, bidx_ref, lam_ref, subg_ref, o_ref,
                     bias_ref, vext_ref, acc_ref, m_ref, *, lam_init):
    head, b, qi = pl.program_id(0), pl.program_id(1), pl.program_id(2)
    far = REL_BUCKETS - 1

    @pl.when((b == 0) & (qi == 0))
    def _():
        row = lax.broadcasted_iota(jnp.int32, (BLOCK, BLOCK), 0)
        col = lax.broadcasted_iota(jnp.int32, (BLOCK, BLOCK), 1)
        bias_ref[0] = jnp.full((BLOCK, BLOCK), NEG_BIG, F32)
        for t in range(2):
            idx = bidx_ref[t]
            bias = jnp.zeros((BLOCK, BLOCK), F32)
            for bk in range(REL_BUCKETS):
                bias = jnp.where(idx == bk, tab_ref[bk, head] - tab_ref[far, head], bias)
            if t == 0:
                bias = jnp.where(col <= row, bias, NEG_BIG)
            bias_ref[1 + t] = bias
        bias_ref[3] = jnp.zeros((BLOCK, BLOCK), F32)

    @pl.when(qi == 0)
    def _():
        vext_ref[:, :LANES] = v_ref[0]
        vext_ref[:, LANES:] = jnp.ones((vext_ref.shape[0], LANES), vext_ref.dtype)

    q = q_ref[0] * (DIFF_DIM ** -0.5)
    lane = lax.broadcasted_iota(jnp.int32, q.shape, 1)
    q_maps = [jnp.where((lane // DIFF_DIM) == p, q, jnp.zeros_like(q)) for p in range(2)]
    acc_ref[...] = jnp.zeros_like(acc_ref)
    m_ref[...] = jnp.full(m_ref.shape, NEG_BIG, F32)

    def step(j, carry, near):
        start = pl.multiple_of(j * DA_TK, DA_TK)
        kt = k_ref[0, pl.ds(start, DA_TK), :]
        vt = vext_ref[pl.ds(start, DA_TK), :]
        if near:
            rows = []
            for rb in range(DA_QB):
                dist = (qi * DA_QB + rb) - j * DA_KB
                rows.append(jnp.concatenate(
                    [bias_ref[jnp.clip(dist - cb, -1, 2) + 1] for cb in range(DA_KB)], axis=1))
            bias = jnp.concatenate(rows, axis=0)
        for p in range(2):
            s = _dot_nt(q_maps[p], kt)
            if near:
                s = s + bias
            m_old = m_ref[p]
            m_new = jnp.maximum(m_old, jnp.max(s, axis=-1, keepdims=True))
            m_ref[p] = m_new
            e = jnp.exp(s - _lane_tile(m_new, DA_KB))
            acc_ref[p] = acc_ref[p] * _lane_tile(jnp.exp(m_old - m_new), 2) + _dot(e.astype(BF16), vt)
        return carry

    n_tiles = (qi * DA_TQ + DA_TQ - 1) // DA_TK + 1
    n_far = jnp.maximum((qi * DA_QB - 1) // DA_KB, 0)
    lax.fori_loop(0, n_far, functools.partial(step, near=False), 0)
    lax.fori_loop(n_far, n_tiles, functools.partial(step, near=True), 0)

    lam_v = lam_ref[...]
    lam = (jnp.exp(jnp.sum(lam_v[0:1] * lam_v[1:2], axis=-1, keepdims=True))
           - jnp.exp(jnp.sum(lam_v[2:3] * lam_v[3:4], axis=-1, keepdims=True)) + lam_init)
    o0 = acc_ref[0, :, :LANES] / acc_ref[0, :, LANES:]
    o1 = acc_ref[1, :, :LANES] / acc_ref[1, :, LANES:]
    o = o0 - lam * o1
    o = o * lax.rsqrt(jnp.mean(o * o, axis=-1, keepdims=True) + LN_EPS) * subg_ref[...]
    o_ref[0] = (o * (1.0 - lam_init)).astype(o_ref.dtype)


def _diffattn(h3, rel_bias, lam_params, sub_g, lam_init):
    bsz, seq, _ = h3.shape
    assert seq % DA_TK == 0 and DA_TK % DA_TQ == 0
    return pl.pallas_call(
        functools.partial(_diffattn_kernel, lam_init=lam_init),
        out_shape=jax.ShapeDtypeStruct((bsz, seq, DIFF_W), BF16),
        grid=(DIFF_HEADS, bsz, seq // DA_TQ),
        in_specs=[
            pl.BlockSpec(memory_space=pltpu.SMEM),
            pl.BlockSpec((1, DA_TQ, LANES), lambda h, b, i: (b, i, h)),
            pl.BlockSpec((1, seq, LANES), lambda h, b, i: (b, 0, DIFF_HEADS + h)),
            pl.BlockSpec((1, seq, LANES), lambda h, b, i: (b, 0, 2 * DIFF_HEADS + h)),
            _resident((2, BLOCK, BLOCK)),
            _resident((4, DIFF_DIM)),
            _resident((1, 2 * DIFF_DIM)),
        ],
        out_specs=pl.BlockSpec((1, DA_TQ, LANES), lambda h, b, i: (b, i, h)),
        scratch_shapes=[pltpu.VMEM((4, BLOCK, BLOCK), F32),
                        pltpu.VMEM((seq, 2 * LANES), BF16),
                        pltpu.VMEM((2, DA_TQ, 2 * LANES), F32),
                        pltpu.VMEM((2, DA_TQ, LANES), F32)],
        compiler_params=_params(("arbitrary", "arbitrary", "arbitrary")),
        name="diffattn",
    )(rel_bias, h3, h3, h3, jnp.asarray(_bucket_tiles()), lam_params, sub_g)


def _proj_ln_kernel(*refs, n_in):
    a_refs, w_refs = refs[:n_in], refs[n_in:2 * n_in]
    x_ref, g_ref, b_ref, o_ref, obf_ref = refs[2 * n_in:]
    y = ALPHA * x_ref[...]
    for a_ref, w_ref in zip(a_refs, w_refs):
        y = y + _dot(a_ref[...], w_ref[...])
    mu = jnp.mean(y, axis=-1, keepdims=True)
    cen = y - mu
    var = jnp.mean(cen * cen, axis=-1, keepdims=True)
    out = cen * lax.rsqrt(var + LN_EPS) * g_ref[...] + b_ref[...]
    o_ref[...] = out
    obf_ref[...] = out.astype(obf_ref.dtype)


def _proj_ln(acts, weights, x, gain, bias, tm=ROW_TILE):
    m, d = x.shape
    assert m % tm == 0
    n_in = len(acts)
    in_specs = [pl.BlockSpec((tm, a.shape[1]), lambda i: (i, 0)) for a in acts]
    in_specs += [_resident(w.shape) for w in weights]
    in_specs += [pl.BlockSpec((tm, d), lambda i: (i, 0)), _resident((1, d)), _resident((1, d))]
    return pl.pallas_call(
        functools.partial(_proj_ln_kernel, n_in=n_in),
        out_shape=(jax.ShapeDtypeStruct((m, d), F32), jax.ShapeDtypeStruct((m, d), BF16)),
        grid=(m // tm,),
        in_specs=in_specs,
        out_specs=(pl.BlockSpec((tm, d), lambda i: (i, 0)), pl.BlockSpec((tm, d), lambda i: (i, 0))),
        compiler_params=_params(("parallel",)),
        name="proj_ln",
    )(*acts, *weights, x, gain.reshape(1, d), bias.reshape(1, d))


def _shift_rows(h, prev, n):
    row = lax.broadcasted_iota(jnp.int32, h.shape, 0)
    out = pltpu.roll(h, n, axis=0)
    for r in range(n):
        out = jnp.where(row == r, prev[SUBLANES - n + r:SUBLANES - n + r + 1], out)
    return out


def _ffn_up_kernel(x_ref, w_ref, cw_ref, cb_ref, o_ref, carry_ref, *, tiles_per_seq):
    x = x_ref[...]
    tm = x.shape[0]
    seq_start = (pl.program_id(0) % tiles_per_seq) == 0
    for c in range(D_FF // MXU_N):
        halves = []
        for base in (c * MXU_N, D_FF + c * MXU_N):
            cols = slice(base, base + MXU_N)
            h = _dot(x, w_ref[:, cols])
            prev = jnp.where(seq_start, 0.0, carry_ref[:, cols])
            carry_ref[:, cols] = h[tm - SUBLANES:]
            cw = cw_ref[:, cols]
            halves.append(_shift_rows(h, prev, 2) * cw[0:1] + _shift_rows(h, prev, 1) * cw[1:2]
                          + h * cw[2:3] + cb_ref[:, cols])
        u, g = halves
        o_ref[:, c * MXU_N:(c + 1) * MXU_N] = (g * jax.nn.sigmoid(g) * u).astype(o_ref.dtype)


def _ffn_up(x_bf, w_up, conv_w, conv_b, seq, tm=ROW_TILE):
    m, d = x_bf.shape
    assert seq % tm == 0
    return pl.pallas_call(
        functools.partial(_ffn_up_kernel, tiles_per_seq=seq // tm),
        out_shape=jax.ShapeDtypeStruct((m, D_FF), BF16),
        grid=(m // tm,),
        in_specs=[pl.BlockSpec((tm, d), lambda i: (i, 0)), _resident((d, 2 * D_FF)),
                  _resident((CONV_W, 2 * D_FF)), _resident((1, 2 * D_FF))],
        out_specs=pl.BlockSpec((tm, D_FF), lambda i: (i, 0)),
        scratch_shapes=[pltpu.VMEM((SUBLANES, 2 * D_FF), F32)],
        compiler_params=_params(("arbitrary",)),
        name="ffn_up",
    )(x_bf, w_up, conv_w, conv_b.reshape(1, 2 * D_FF))


def _rotary_tables(seq):
    inv = ROPE_BASE ** (-jnp.arange(0, RET_DIM, 2, dtype=F32) / RET_DIM)
    ang = jnp.arange(seq, dtype=F32)[:, None] * inv[None, :]
    cos, sin = jnp.cos(ang), jnp.sin(ang)
    return jnp.concatenate([cos, cos], -1), jnp.concatenate([-sin, sin], -1)


def kernel(x, w_in_even, w_out_even, w_in_odd, w_out_odd, lam_q1, lam_k1, lam_q2, lam_k2, subln_g,
           rel_bias, w_up, conv_w, conv_b, w_down, ln1_g, ln1_b, ln2_g, ln2_b):
    bsz, seq, d = x.shape
    m = bsz * seq
    cos_t, sin_t = _rotary_tables(seq)
    xf = x.reshape(m, d)
    xb = xf.astype(BF16)
    for l in range(DEPTH):
        i = l // 2
        if l % 2 == 0:
            h = _in_proj(xb, w_in_even[i].astype(BF16), cos_t, sin_t, seq, rotary_cols=2 * RET_W)
            h3 = h.reshape(bsz, seq, EVEN_IN)
            ret = _retention(h3).reshape(m, RET_W)
            sb = _stickbreak(h3).reshape(m, SB_W)
            w_o = w_out_even[i].astype(BF16)
            xf, xb = _proj_ln([ret, sb], [w_o[:RET_W], w_o[RET_W:]], xf, ln1_g[l], ln1_b[l])
        else:
            h = _in_proj(xb, w_in_odd[i].astype(BF16), cos_t, sin_t, seq, rotary_cols=0)
            lam_params = jnp.stack([lam_q1[i], lam_k1[i], lam_q2[i], lam_k2[i]]).astype(F32)
            lam_init = 0.8 - 0.6 * math.exp(-0.3 * l)
            o = _diffattn(h.reshape(bsz, seq, ODD_IN), rel_bias.astype(F32), lam_params,
                          subln_g[i].reshape(1, 2 * DIFF_DIM).astype(F32), lam_init)
            xf, xb = _proj_ln([o.reshape(m, DIFF_W)], [w_out_odd[i].astype(BF16)], xf, ln1_g[l], ln1_b[l])
        f = _ffn_up(xb, w_up[l].astype(BF16), conv_w[l], conv_b[l], seq)
        xf, xb = _proj_ln([f], [w_down[l].astype(BF16)], xf, ln2_g[l], ln2_b[l])
    return xf.reshape(bsz, seq, d)
```

```python
import functools
import math

import numpy as np
import jax
import jax.numpy as jnp
from jax import lax
from jax.experimental import pallas as pl
from jax.experimental.pallas import tpu as pltpu

D_MODEL = 1024
DEPTH = 4
BLOCK = 128
RET_HEADS = 4
RET_DIM = 128
SB_HEADS = 8
SB_DIM = 64
DIFF_HEADS = 8
DIFF_DIM = 64
REL_BUCKETS = 32
REL_MAX_DIST = 128
D_FF = 2816
CONV_W = 3
ALPHA = (2 * DEPTH) ** 0.25
LN_EPS = 1e-5
ROPE_BASE = 10000.0

RET_W = RET_HEADS * RET_DIM
SB_W = SB_HEADS * SB_DIM
EVEN_IN = 4 * RET_W + 3 * SB_W
DIFF_W = DIFF_HEADS * 2 * DIFF_DIM
ODD_IN = 3 * DIFF_W

LANES = 128
SUBLANES = 8
MXU_N = 256
VMEM_LIMIT = 56 * 1024 * 1024
NEG_BIG = -1e30
ROW_TILE = 512

F32 = jnp.float32
BF16 = jnp.bfloat16


def _params(semantics):
    return pltpu.CompilerParams(dimension_semantics=semantics, vmem_limit_bytes=VMEM_LIMIT)


def _resident(shape):
    return pl.BlockSpec(shape, lambda *_: (0,) * len(shape), pipeline_mode=pl.Buffered(1))


def _dot(a, b):
    return jnp.dot(a, b, preferred_element_type=F32)


def _dot_nt(a, b):
    return lax.dot_general(a, b, (((1,), (1,)), ((), ())), preferred_element_type=F32)


def _dot_tn(a, b):
    return lax.dot_general(a, b, (((0,), (0,)), ((), ())), preferred_element_type=F32)


def _lane_tile(x, reps):
    return x if reps == 1 else jnp.concatenate([x] * reps, axis=1)


def _stack_heads(q, width):
    lane = lax.broadcasted_iota(jnp.int32, q.shape, 1)
    zero = jnp.zeros_like(q)
    return jnp.concatenate([jnp.where((lane // width) == p, q, zero) for p in range(LANES // width)], axis=0)


IN_CHUNK = 512


def _in_proj_kernel(x_ref, w_ref, cos_ref, sin_ref, o_ref, *, n_out, rotary_cols):
    x = x_ref[...]
    for c in range(n_out // IN_CHUNK):
        lo = c * IN_CHUNK
        acc = _dot(x, w_ref[:, lo:lo + IN_CHUNK])
        if lo < rotary_cols:
            cosv, sinv = cos_ref[...], sin_ref[...]
            for hh in range(IN_CHUNK // RET_DIM):
                blk = acc[:, hh * RET_DIM:(hh + 1) * RET_DIM]
                rot = blk * cosv + pltpu.roll(blk, RET_DIM // 2, axis=1) * sinv
                o_ref[:, lo + hh * RET_DIM:lo + (hh + 1) * RET_DIM] = rot.astype(o_ref.dtype)
        else:
            o_ref[:, lo:lo + IN_CHUNK] = acc.astype(o_ref.dtype)


def _in_proj(x_bf, w, cos_t, sin_t, seq, rotary_cols, tm=ROW_TILE):
    m, k = x_bf.shape
    n_out = w.shape[1]
    assert seq % tm == 0 and n_out % IN_CHUNK == 0
    s_tiles = seq // tm
    return pl.pallas_call(
        functools.partial(_in_proj_kernel, n_out=n_out, rotary_cols=rotary_cols),
        out_shape=jax.ShapeDtypeStruct((m, n_out), BF16),
        grid=(m // tm,),
        in_specs=[
            pl.BlockSpec((tm, k), lambda i: (i, 0)),
            _resident((k, n_out)),
            pl.BlockSpec((tm, RET_DIM), lambda i: (i % s_tiles, 0)),
            pl.BlockSpec((tm, RET_DIM), lambda i: (i % s_tiles, 0)),
        ],
        out_specs=pl.BlockSpec((tm, n_out), lambda i: (i, 0)),
        compiler_params=_params(("parallel",)),
        name="in_proj",
    )(x_bf, w, cos_t, sin_t)


def _retention_tables():
    hh = np.arange(RET_HEADS, dtype=np.float32)
    log_g = np.log(np.float32(1.0) - np.float32(2.0) ** (np.float32(-5.0) - hh)).astype(np.float32)
    idx = np.arange(BLOCK, dtype=np.float32)
    rel = idx[:, None] - idx[None, :]
    scale = np.float32(RET_DIM ** -0.5)
    decay = np.where(rel >= 0, np.exp(log_g[:, None, None] * np.maximum(rel, 0.0)), 0.0) * scale
    q_dec = np.exp(log_g[:, None] * (idx[None, :] + 1.0))
    k_dec = np.exp(log_g[:, None] * (BLOCK - 1.0 - idx[None, :])) * scale
    chunk_g = np.exp(log_g * BLOCK)
    bcast = lambda t: np.broadcast_to(t[:, :, None], (RET_HEADS, BLOCK, LANES))
    return (decay.astype(np.float32), np.ascontiguousarray(bcast(q_dec), np.float32),
            np.ascontiguousarray(bcast(k_dec), np.float32), [float(g) for g in chunk_g])


def _retention_kernel(q_ref, k_ref, v_ref, g_ref, decay_ref, qdec_ref, kdec_ref, o_ref, state_ref,
                      *, chunk_g):
    @pl.when(pl.program_id(1) == 0)
    def _():
        state_ref[...] = jnp.zeros_like(state_ref)

    for hh in range(RET_HEADS):
        cols = slice(hh * RET_DIM, (hh + 1) * RET_DIM)
        q, k, v = q_ref[0, :, cols], k_ref[0, :, cols], v_ref[0, :, cols]
        state = state_ref[hh]
        s = _dot_nt(q, k) * decay_ref[hh]
        out = _dot(s.astype(BF16), v) + _dot(q, state.astype(BF16)) * qdec_ref[hh]
        k_scaled = (k.astype(F32) * kdec_ref[hh]).astype(BF16)
        state_ref[hh] = state * chunk_g[hh] + _dot_tn(k_scaled, v)
        mu = jnp.mean(out, axis=-1, keepdims=True)
        cen = out - mu
        var = jnp.mean(cen * cen, axis=-1, keepdims=True)
        gate = g_ref[0, :, cols].astype(F32)
        o_ref[0, :, cols] = (cen * lax.rsqrt(var + LN_EPS) * (gate * jax.nn.sigmoid(gate))).astype(o_ref.dtype)


def _retention(h3):
    bsz, seq, _ = h3.shape
    decay, q_dec, k_dec, chunk_g = _retention_tables()
    col = lambda j: pl.BlockSpec((1, BLOCK, RET_W), lambda b, c, j=j: (b, c, j))
    return pl.pallas_call(
        functools.partial(_retention_kernel, chunk_g=chunk_g),
        out_shape=jax.ShapeDtypeStruct((bsz, seq, RET_W), BF16),
        grid=(bsz, seq // BLOCK),
        in_specs=[col(0), col(1), col(2), col(3),
                  _resident((RET_HEADS, BLOCK, BLOCK)), _resident((RET_HEADS, BLOCK, LANES)),
                  _resident((RET_HEADS, BLOCK, LANES))],
        out_specs=pl.BlockSpec((1, BLOCK, RET_W), lambda b, c: (b, c, 0)),
        scratch_shapes=[pltpu.VMEM((RET_HEADS, RET_DIM, RET_DIM), F32)],
        compiler_params=_params(("parallel", "arbitrary")),
        name="retention",
    )(h3, h3, h3, h3, jnp.asarray(decay), jnp.asarray(q_dec), jnp.asarray(k_dec))


SB_PAIR = LANES // SB_DIM
SB_KEYS = MXU_N
SB_DEAD = -104.0


def _split_bf16(x):
    hi = x.astype(BF16)
    lo = (x - hi.astype(F32)).astype(BF16)
    return hi, lo


def _stickbreak_kernel(q_ref, k_ref, v_ref, tri_ref, o_ref, acc_ref, run_ref):
    qi = pl.program_id(2)
    q = q_ref[0] * (SB_DIM ** -0.5)
    lane = lax.broadcasted_iota(jnp.int32, q.shape, 1)
    q_heads = [jnp.where((lane // SB_DIM) == p, q, jnp.zeros_like(q)) for p in range(SB_PAIR)]
    tri = tri_ref[...]

    def add_block(jb, runs, accs, diagonal=False, valid=None):
        start = pl.multiple_of(jb * SB_KEYS, SB_KEYS)
        kt = k_ref[0, pl.ds(start, SB_KEYS), :]
        vt = v_ref[0, pl.ds(start, SB_KEYS), :]
        if diagonal:
            r = lax.broadcasted_iota(jnp.int32, (SB_KEYS, SB_KEYS), 0)
            c = lax.broadcasted_iota(jnp.int32, (SB_KEYS, SB_KEYS), 1)
            past = c < r
        new_runs, new_accs = [], []
        for p in range(SB_PAIR):
            z = _dot_nt(q_heads[p], kt)
            log_take = jnp.minimum(z, 0.0) - jnp.log(1.0 + jnp.exp(-jnp.abs(z)))
            log_keep = log_take - z
            if diagonal:
                log_keep = jnp.where(past, log_keep, 0.0)
            if valid is not None:
                log_keep = jnp.where(valid, log_keep, 0.0)
            hi, lo = _split_bf16(log_keep)
            between = _dot(hi, tri) + _dot(lo, tri) + _lane_tile(runs[p], SB_KEYS // LANES)
            a = jnp.exp(log_take + between)
            if diagonal:
                a = jnp.where(past, a, 0.0)
            if valid is not None:
                a = jnp.where(valid, a, 0.0)
            new_accs.append(accs[p] + _dot(a.astype(BF16), vt))
            new_runs.append(runs[p] + jnp.sum(log_keep, axis=-1, keepdims=True))
        return new_runs, new_accs

    def any_alive(runs):
        return jnp.max(functools.reduce(jnp.maximum, runs)) > SB_DEAD

    zeros = [jnp.zeros((SB_KEYS, LANES), F32)] * SB_PAIR
    runs, accs = add_block(qi, zeros, zeros, diagonal=True)
    runs, accs = add_block(jnp.maximum(qi - 1, 0), runs, accs, valid=qi > 0)
    for p in range(SB_PAIR):
        run_ref[p], acc_ref[p] = runs[p], accs[p]

    def more(carry):
        jb, _ = carry
        runs, accs = add_block(jb, [run_ref[p] for p in range(SB_PAIR)],
                               [acc_ref[p] for p in range(SB_PAIR)])
        for p in range(SB_PAIR):
            run_ref[p], acc_ref[p] = runs[p], accs[p]
        return jb - 1, any_alive(runs)

    lax.while_loop(lambda c: (c[0] >= 0) & c[1], more, (qi - 2, any_alive(runs)))
    out = acc_ref[0]
    for p in range(1, SB_PAIR):
        out = jnp.where((lane // SB_DIM) == p, acc_ref[p], out)
    o_ref[0] = out.astype(o_ref.dtype)


def _stickbreak(h3):
    bsz, seq, _ = h3.shape
    assert seq % SB_KEYS == 0
    n_pairs = SB_W // LANES
    base = 4 * RET_W // LANES
    idx = np.arange(SB_KEYS)
    tri = jnp.asarray((idx[:, None] > idx[None, :]).astype(np.float32), BF16)
    return pl.pallas_call(
        _stickbreak_kernel,
        out_shape=jax.ShapeDtypeStruct((bsz, seq, SB_W), BF16),
        grid=(bsz, n_pairs, seq // SB_KEYS),
        in_specs=[
            pl.BlockSpec((1, SB_KEYS, LANES), lambda b, p, i: (b, i, base + p)),
            pl.BlockSpec((1, seq, LANES), lambda b, p, i: (b, 0, base + n_pairs + p)),
            pl.BlockSpec((1, seq, LANES), lambda b, p, i: (b, 0, base + 2 * n_pairs + p)),
            _resident((SB_KEYS, SB_KEYS)),
        ],
        out_specs=pl.BlockSpec((1, SB_KEYS, LANES), lambda b, p, i: (b, i, p)),
        scratch_shapes=[pltpu.VMEM((SB_PAIR, SB_KEYS, LANES), F32),
                        pltpu.VMEM((SB_PAIR, SB_KEYS, LANES), F32)],
        compiler_params=_params(("parallel", "parallel", "arbitrary")),
        name="stickbreak",
    )(h3, h3, h3, tri)


DA_TQ = 512
DA_TK = 512
DA_QB = DA_TQ // BLOCK
DA_KB = DA_TK // BLOCK


def _bucket_tiles():
    rel = np.arange(2 * BLOCK)
    max_exact = REL_BUCKETS // 2
    nf = np.maximum(rel, 1).astype(np.float32)
    large = max_exact + (np.log(nf / np.float32(max_exact)) / np.float32(math.log(REL_MAX_DIST / max_exact))
                         * np.float32(REL_BUCKETS - max_exact)).astype(np.int32)
    bucket = np.where(rel < max_exact, rel, np.minimum(large, REL_BUCKETS - 1)).astype(np.int32)
    assert bucket[BLOCK - 1:].min() == REL_BUCKETS - 1
    idx = np.arange(BLOCK)
    d = idx[:, None] - idx[None, :]
    return np.stack([bucket[np.maximum(d, 0)], bucket[d + BLOCK]]).astype(np.int32)


def _diffattn_kernel(tab_ref, q_ref, k_ref, v_ref, bidx_ref, lam_ref, subg_ref, o_ref,
                     bias_ref, vext_ref, acc_ref, m_ref, s_ref, *, lam_init):
    head, b, qi = pl.program_id(0), pl.program_id(1), pl.program_id(2)
    far = REL_BUCKETS - 1

    @pl.when((b == 0) & (qi == 0))
    def _():
        row = lax.broadcasted_iota(jnp.int32, (BLOCK, BLOCK), 0)
        col = lax.broadcasted_iota(jnp.int32, (BLOCK, BLOCK), 1)
        bias_ref[0] = jnp.full((BLOCK, BLOCK), NEG_BIG, F32)
        for t in range(2):
            idx = bidx_ref[t]
            bias = jnp.zeros((BLOCK, BLOCK), F32)
            for bk in range(REL_BUCKETS):
                bias = jnp.where(idx == bk, tab_ref[bk, head] - tab_ref[far, head], bias)
            if t == 0:
                bias = jnp.where(col <= row, bias, NEG_BIG)
            bias_ref[1 + t] = bias
        bias_ref[3] = jnp.zeros((BLOCK, BLOCK), F32)

    @pl.when(qi == 0)
    def _():
        vext_ref[:, :LANES] = v_ref[0]
        vext_ref[:, LANES:] = jnp.ones((vext_ref.shape[0], LANES), vext_ref.dtype)

    q = q_ref[0] * (DIFF_DIM ** -0.5)
    lane = lax.broadcasted_iota(jnp.int32, q.shape, 1)
    q_maps = [jnp.where((lane // DIFF_DIM) == p, q, jnp.zeros_like(q)) for p in range(2)]
    acc_ref[...] = jnp.zeros_like(acc_ref)
    m_ref[...] = jnp.full(m_ref.shape, NEG_BIG, F32)

    def scores(j):
        start = pl.multiple_of(j * DA_TK, DA_TK)
        kt = k_ref[0, pl.ds(start, DA_TK), :]
        for p in range(2):
            s_ref[p] = _dot_nt(q_maps[p], kt)

    def absorb(j, near):
        start = pl.multiple_of(j * DA_TK, DA_TK)
        vt = vext_ref[pl.ds(start, DA_TK), :]
        if near:
            rows = []
            for rb in range(DA_QB):
                dist = (qi * DA_QB + rb) - j * DA_KB
                rows.append(jnp.concatenate(
                    [bias_ref[jnp.clip(dist - cb, -1, 2) + 1] for cb in range(DA_KB)], axis=1))
            bias = jnp.concatenate(rows, axis=0)
        for p in range(2):
            s = s_ref[p] + bias if near else s_ref[p]
            m_old = m_ref[p]
            m_new = jnp.maximum(m_old, jnp.max(s, axis=-1, keepdims=True))
            m_ref[p] = m_new
            e = jnp.exp(s - _lane_tile(m_new, DA_KB))
            acc_ref[p] = acc_ref[p] * _lane_tile(jnp.exp(m_old - m_new), 2) + _dot(e.astype(BF16), vt)

    def step(j, carry, near):
        absorb(j, near)
        scores(j + 1)
        return carry

    n_tiles = (qi * DA_TQ + DA_TQ - 1) // DA_TK + 1
    n_far = jnp.maximum((qi * DA_QB - 1) // DA_KB, 0)
    scores(0)
    lax.fori_loop(0, n_far, functools.partial(step, near=False), 0)
    lax.fori_loop(n_far, n_tiles - 1, functools.partial(step, near=True), 0)
    absorb(n_tiles - 1, True)

    lam_v = lam_ref[...]
    lam = (jnp.exp(jnp.sum(lam_v[0:1] * lam_v[1:2], axis=-1, keepdims=True))
           - jnp.exp(jnp.sum(lam_v[2:3] * lam_v[3:4], axis=-1, keepdims=True)) + lam_init)
    o0 = acc_ref[0, :, :LANES] / acc_ref[0, :, LANES:]
    o1 = acc_ref[1, :, :LANES] / acc_ref[1, :, LANES:]
    o = o0 - lam * o1
    o = o * lax.rsqrt(jnp.mean(o * o, axis=-1, keepdims=True) + LN_EPS) * subg_ref[...]
    o_ref[0] = (o * (1.0 - lam_init)).astype(o_ref.dtype)


def _diffattn(h3, rel_bias, lam_params, sub_g, lam_init):
    bsz, seq, _ = h3.shape
    assert seq % DA_TK == 0 and DA_TK % DA_TQ == 0
    return pl.pallas_call(
        functools.partial(_diffattn_kernel, lam_init=lam_init),
        out_shape=jax.ShapeDtypeStruct((bsz, seq, DIFF_W), BF16),
        grid=(DIFF_HEADS, bsz, seq // DA_TQ),
        in_specs=[
            pl.BlockSpec(memory_space=pltpu.SMEM),
            pl.BlockSpec((1, DA_TQ, LANES), lambda h, b, i: (b, i, h)),
            pl.BlockSpec((1, seq, LANES), lambda h, b, i: (b, 0, DIFF_HEADS + h)),
            pl.BlockSpec((1, seq, LANES), lambda h, b, i: (b, 0, 2 * DIFF_HEADS + h)),
            _resident((2, BLOCK, BLOCK)),
            _resident((4, DIFF_DIM)),
            _resident((1, 2 * DIFF_DIM)),
        ],
        out_specs=pl.BlockSpec((1, DA_TQ, LANES), lambda h, b, i: (b, i, h)),
        scratch_shapes=[pltpu.VMEM((4, BLOCK, BLOCK), F32),
                        pltpu.VMEM((seq, 2 * LANES), BF16),
                        pltpu.VMEM((2, DA_TQ, 2 * LANES), F32),
                        pltpu.VMEM((2, DA_TQ, LANES), F32),
                        pltpu.VMEM((2, DA_TQ, DA_TK), F32)],
        compiler_params=_params(("arbitrary", "arbitrary", "arbitrary")),
        name="diffattn",
    )(rel_bias, h3, h3, h3, jnp.asarray(_bucket_tiles()), lam_params, sub_g)


def _proj_ln_kernel(*refs, n_in):
    a_refs, w_refs = refs[:n_in], refs[n_in:2 * n_in]
    x_ref, g_ref, b_ref, o_ref, obf_ref = refs[2 * n_in:]
    y = ALPHA * x_ref[...]
    for a_ref, w_ref in zip(a_refs, w_refs):
        y = y + _dot(a_ref[...], w_ref[...])
    mu = jnp.mean(y, axis=-1, keepdims=True)
    cen = y - mu
    var = jnp.mean(cen * cen, axis=-1, keepdims=True)
    out = cen * lax.rsqrt(var + LN_EPS) * g_ref[...] + b_ref[...]
    o_ref[...] = out
    obf_ref[...] = out.astype(obf_ref.dtype)


def _proj_ln(acts, weights, x, gain, bias, tm=ROW_TILE):
    m, d = x.shape
    assert m % tm == 0
    n_in = len(acts)
    in_specs = [pl.BlockSpec((tm, a.shape[1]), lambda i: (i, 0)) for a in acts]
    in_specs += [_resident(w.shape) for w in weights]
    in_specs += [pl.BlockSpec((tm, d), lambda i: (i, 0)), _resident((1, d)), _resident((1, d))]
    return pl.pallas_call(
        functools.partial(_proj_ln_kernel, n_in=n_in),
        out_shape=(jax.ShapeDtypeStruct((m, d), F32), jax.ShapeDtypeStruct((m, d), BF16)),
        grid=(m // tm,),
        in_specs=in_specs,
        out_specs=(pl.BlockSpec((tm, d), lambda i: (i, 0)), pl.BlockSpec((tm, d), lambda i: (i, 0))),
        compiler_params=_params(("parallel",)),
        name="proj_ln",
    )(*acts, *weights, x, gain.reshape(1, d), bias.reshape(1, d))


def _shift_rows(h, prev, n):
    tm, width = h.shape
    rot = pltpu.roll(h.reshape(tm // SUBLANES, SUBLANES, width), n, axis=1)
    before = jnp.concatenate([pltpu.roll(prev, n, axis=0)[None], rot[:-1]], axis=0)
    sub = lax.broadcasted_iota(jnp.int32, rot.shape, 1)
    return jnp.where(sub < n, before, rot).reshape(tm, width)


def _ffn_up_kernel(x_ref, w_ref, cw_ref, cb_ref, o_ref, carry_ref, *, tiles_per_seq):
    x = x_ref[...]
    tm = x.shape[0]
    seq_start = (pl.program_id(0) % tiles_per_seq) == 0
    for c in range(D_FF // MXU_N):
        halves = []
        for base in (c * MXU_N, D_FF + c * MXU_N):
            cols = slice(base, base + MXU_N)
            h = _dot(x, w_ref[:, cols])
            prev = jnp.where(seq_start, 0.0, carry_ref[:, cols])
            carry_ref[:, cols] = h[tm - SUBLANES:]
            cw = cw_ref[:, cols]
            halves.append(_shift_rows(h, prev, 2) * cw[0:1] + _shift_rows(h, prev, 1) * cw[1:2]
                          + h * cw[2:3] + cb_ref[:, cols])
        u, g = halves
        o_ref[:, c * MXU_N:(c + 1) * MXU_N] = (g * jax.nn.sigmoid(g) * u).astype(o_ref.dtype)


def _ffn_up(x_bf, w_up, conv_w, conv_b, seq, tm=ROW_TILE):
    m, d = x_bf.shape
    assert seq % tm == 0
    return pl.pallas_call(
        functools.partial(_ffn_up_kernel, tiles_per_seq=seq // tm),
        out_shape=jax.ShapeDtypeStruct((m, D_FF), BF16),
        grid=(m // tm,),
        in_specs=[pl.BlockSpec((tm, d), lambda i: (i, 0)), _resident((d, 2 * D_FF)),
                  _resident((CONV_W, 2 * D_FF)), _resident((1, 2 * D_FF))],
        out_specs=pl.BlockSpec((tm, D_FF), lambda i: (i, 0)),
        scratch_shapes=[pltpu.VMEM((SUBLANES, 2 * D_FF), F32)],
        compiler_params=_params(("arbitrary",)),
        name="ffn_up",
    )(x_bf, w_up, conv_w, conv_b.reshape(1, 2 * D_FF))


def _rotary_tables(seq):
    inv = ROPE_BASE ** (-jnp.arange(0, RET_DIM, 2, dtype=F32) / RET_DIM)
    ang = jnp.arange(seq, dtype=F32)[:, None] * inv[None, :]
    cos, sin = jnp.cos(ang), jnp.sin(ang)
    return jnp.concatenate([cos, cos], -1), jnp.concatenate([-sin, sin], -1)


def kernel(x, w_in_even, w_out_even, w_in_odd, w_out_odd, lam_q1, lam_k1, lam_q2, lam_k2, subln_g,
           rel_bias, w_up, conv_w, conv_b, w_down, ln1_g, ln1_b, ln2_g, ln2_b):
    bsz, seq, d = x.shape
    m = bsz * seq
    cos_t, sin_t = _rotary_tables(seq)
    xf = x.reshape(m, d)
    xb = xf.astype(BF16)
    for l in range(DEPTH):
        i = l // 2
        if l % 2 == 0:
            h = _in_proj(xb, w_in_even[i].astype(BF16), cos_t, sin_t, seq, rotary_cols=2 * RET_W)
            h3 = h.reshape(bsz, seq, EVEN_IN)
            ret = _retention(h3).reshape(m, RET_W)
            sb = _stickbreak(h3).reshape(m, SB_W)
            w_o = w_out_even[i].astype(BF16)
            xf, xb = _proj_ln([ret, sb], [w_o[:RET_W], w_o[RET_W:]], xf, ln1_g[l], ln1_b[l])
        else:
            h = _in_proj(xb, w_in_odd[i].astype(BF16), cos_t, sin_t, seq, rotary_cols=0)
            lam_params = jnp.stack([lam_q1[i], lam_k1[i], lam_q2[i], lam_k2[i]]).astype(F32)
            lam_init = 0.8 - 0.6 * math.exp(-0.3 * l)
            o = _diffattn(h.reshape(bsz, seq, ODD_IN), rel_bias.astype(F32), lam_params,
                          subln_g[i].reshape(1, 2 * DIFF_DIM).astype(F32), lam_init)
            xf, xb = _proj_ln([o.reshape(m, DIFF_W)], [w_out_odd[i].astype(BF16)], xf, ln1_g[l], ln1_b[l])
        f = _ffn_up(xb, w_up[l].astype(BF16), conv_w[l], conv_b[l], seq)
        xf, xb = _proj_ln([f], [w_down[l].astype(BF16)], xf, ln2_g[l], ln2_b[l])
    return xf.reshape(bsz, seq, d)
```

```python
import functools
import math

import numpy as np
import jax
import jax.numpy as jnp
from jax import lax
from jax.experimental import pallas as pl
from jax.experimental.pallas import tpu as pltpu

D_MODEL = 1024
DEPTH = 4
BLOCK = 128
RET_HEADS = 4
RET_DIM = 128
SB_HEADS = 8
SB_DIM = 64
DIFF_HEADS = 8
DIFF_DIM = 64
REL_BUCKETS = 32
REL_MAX_DIST = 128
D_FF = 2816
CONV_W = 3
ALPHA = (2 * DEPTH) ** 0.25
LN_EPS = 1e-5
ROPE_BASE = 10000.0

RET_W = RET_HEADS * RET_DIM
SB_W = SB_HEADS * SB_DIM
EVEN_IN = 4 * RET_W + 3 * SB_W
DIFF_W = DIFF_HEADS * 2 * DIFF_DIM
ODD_IN = 3 * DIFF_W

LANES = 128
SUBLANES = 8
MXU_N = 256
VMEM_LIMIT = 56 * 1024 * 1024
NEG_BIG = -1e30
ROW_TILE = 512

F32 = jnp.float32
BF16 = jnp.bfloat16


def _params(semantics):
    return pltpu.CompilerParams(dimension_semantics=semantics, vmem_limit_bytes=VMEM_LIMIT)


def _resident(shape):
    return pl.BlockSpec(shape, lambda *_: (0,) * len(shape), pipeline_mode=pl.Buffered(1))


def _dot(a, b):
    return jnp.dot(a, b, preferred_element_type=F32)


def _dot_nt(a, b):
    return lax.dot_general(a, b, (((1,), (1,)), ((), ())), preferred_element_type=F32)


def _dot_tn(a, b):
    return lax.dot_general(a, b, (((0,), (0,)), ((), ())), preferred_element_type=F32)


def _lane_tile(x, reps):
    return x if reps == 1 else jnp.concatenate([x] * reps, axis=1)


def _stack_heads(q, width):
    lane = lax.broadcasted_iota(jnp.int32, q.shape, 1)
    zero = jnp.zeros_like(q)
    return jnp.concatenate([jnp.where((lane // width) == p, q, zero) for p in range(LANES // width)], axis=0)


IN_CHUNK = 512


def _in_proj_kernel(x_ref, w_ref, cos_ref, sin_ref, o_ref, *, n_out, rotary_cols):
    x = x_ref[...]
    for c in range(n_out // IN_CHUNK):
        lo = c * IN_CHUNK
        acc = _dot(x, w_ref[:, lo:lo + IN_CHUNK])
        if lo < rotary_cols:
            cosv, sinv = cos_ref[...], sin_ref[...]
            for hh in range(IN_CHUNK // RET_DIM):
                blk = acc[:, hh * RET_DIM:(hh + 1) * RET_DIM]
                rot = blk * cosv + pltpu.roll(blk, RET_DIM // 2, axis=1) * sinv
                o_ref[:, lo + hh * RET_DIM:lo + (hh + 1) * RET_DIM] = rot.astype(o_ref.dtype)
        else:
            o_ref[:, lo:lo + IN_CHUNK] = acc.astype(o_ref.dtype)


def _in_proj(x_bf, w, cos_t, sin_t, seq, rotary_cols, tm=ROW_TILE):
    m, k = x_bf.shape
    n_out = w.shape[1]
    assert seq % tm == 0 and n_out % IN_CHUNK == 0
    s_tiles = seq // tm
    return pl.pallas_call(
        functools.partial(_in_proj_kernel, n_out=n_out, rotary_cols=rotary_cols),
        out_shape=jax.ShapeDtypeStruct((m, n_out), BF16),
        grid=(m // tm,),
        in_specs=[
            pl.BlockSpec((tm, k), lambda i: (i, 0)),
            _resident((k, n_out)),
            pl.BlockSpec((tm, RET_DIM), lambda i: (i % s_tiles, 0)),
            pl.BlockSpec((tm, RET_DIM), lambda i: (i % s_tiles, 0)),
        ],
        out_specs=pl.BlockSpec((tm, n_out), lambda i: (i, 0)),
        compiler_params=_params(("parallel",)),
        name="in_proj",
    )(x_bf, w, cos_t, sin_t)


def _retention_tables():
    hh = np.arange(RET_HEADS, dtype=np.float32)
    log_g = np.log(np.float32(1.0) - np.float32(2.0) ** (np.float32(-5.0) - hh)).astype(np.float32)
    idx = np.arange(BLOCK, dtype=np.float32)
    rel = idx[:, None] - idx[None, :]
    scale = np.float32(RET_DIM ** -0.5)
    decay = np.where(rel >= 0, np.exp(log_g[:, None, None] * np.maximum(rel, 0.0)), 0.0) * scale
    q_dec = np.exp(log_g[:, None] * (idx[None, :] + 1.0))
    k_dec = np.exp(log_g[:, None] * (BLOCK - 1.0 - idx[None, :])) * scale
    chunk_g = np.exp(log_g * BLOCK)
    bcast = lambda t: np.broadcast_to(t[:, :, None], (RET_HEADS, BLOCK, LANES))
    return (decay.astype(np.float32), np.ascontiguousarray(bcast(q_dec), np.float32),
            np.ascontiguousarray(bcast(k_dec), np.float32), [float(g) for g in chunk_g])


def _retention_kernel(q_ref, k_ref, v_ref, g_ref, decay_ref, qdec_ref, kdec_ref, o_ref, state_ref,
                      *, chunk_g):
    @pl.when(pl.program_id(1) == 0)
    def _():
        state_ref[...] = jnp.zeros_like(state_ref)

    for hh in range(RET_HEADS):
        cols = slice(hh * RET_DIM, (hh + 1) * RET_DIM)
        q, k, v = q_ref[0, :, cols], k_ref[0, :, cols], v_ref[0, :, cols]
        state = state_ref[hh]
        s = _dot_nt(q, k) * decay_ref[hh]
        out = _dot(s.astype(BF16), v) + _dot(q, state.astype(BF16)) * qdec_ref[hh]
        k_scaled = (k.astype(F32) * kdec_ref[hh]).astype(BF16)
        state_ref[hh] = state * chunk_g[hh] + _dot_tn(k_scaled, v)
        mu = jnp.mean(out, axis=-1, keepdims=True)
        cen = out - mu
        var = jnp.mean(cen * cen, axis=-1, keepdims=True)
        gate = g_ref[0, :, cols].astype(F32)
        o_ref[0, :, cols] = (cen * lax.rsqrt(var + LN_EPS) * (gate * jax.nn.sigmoid(gate))).astype(o_ref.dtype)


def _retention(h3):
    bsz, seq, _ = h3.shape
    decay, q_dec, k_dec, chunk_g = _retention_tables()
    col = lambda j: pl.BlockSpec((1, BLOCK, RET_W), lambda b, c, j=j: (b, c, j))
    return pl.pallas_call(
        functools.partial(_retention_kernel, chunk_g=chunk_g),
        out_shape=jax.ShapeDtypeStruct((bsz, seq, RET_W), BF16),
        grid=(bsz, seq // BLOCK),
        in_specs=[col(0), col(1), col(2), col(3),
                  _resident((RET_HEADS, BLOCK, BLOCK)), _resident((RET_HEADS, BLOCK, LANES)),
                  _resident((RET_HEADS, BLOCK, LANES))],
        out_specs=pl.BlockSpec((1, BLOCK, RET_W), lambda b, c: (b, c, 0)),
        scratch_shapes=[pltpu.VMEM((RET_HEADS, RET_DIM, RET_DIM), F32)],
        compiler_params=_params(("parallel", "arbitrary")),
        name="retention",
    )(h3, h3, h3, h3, jnp.asarray(decay), jnp.asarray(q_dec), jnp.asarray(k_dec))


SB_PAIR = LANES // SB_DIM
SB_KEYS = MXU_N
SB_DEAD = -104.0


def _split_bf16(x):
    hi = x.astype(BF16)
    lo = (x - hi.astype(F32)).astype(BF16)
    return hi, lo


def _stickbreak_kernel(q_ref, k_ref, v_ref, tri_ref, o_ref, acc_ref, run_ref):
    qi = pl.program_id(2)
    q = q_ref[0] * (SB_DIM ** -0.5)
    lane = lax.broadcasted_iota(jnp.int32, q.shape, 1)
    q_heads = [jnp.where((lane // SB_DIM) == p, q, jnp.zeros_like(q)) for p in range(SB_PAIR)]
    tri = tri_ref[...]

    r = lax.broadcasted_iota(jnp.int32, (SB_KEYS, SB_KEYS), 0)
    c = lax.broadcasted_iota(jnp.int32, (SB_KEYS, SB_KEYS), 1)
    past = c < r

    def add_blocks(blocks, runs, accs):
        kts, vts = [], []
        for jb, _, _ in blocks:
            start = pl.multiple_of(jb * SB_KEYS, SB_KEYS)
            kts.append(k_ref[0, pl.ds(start, SB_KEYS), :])
            vts.append(v_ref[0, pl.ds(start, SB_KEYS), :])
        chains = [(b, p) for b in range(len(blocks)) for p in range(SB_PAIR)]

        def masked(b, x):
            _, diagonal, valid = blocks[b]
            if diagonal:
                x = jnp.where(past, x, 0.0)
            if valid is not None:
                x = jnp.where(valid, x, 0.0)
            return x

        z = {bp: _dot_nt(q_heads[bp[1]], kts[bp[0]]) for bp in chains}
        log_take = {bp: jnp.minimum(z[bp], 0.0) - jnp.log(1.0 + jnp.exp(-jnp.abs(z[bp])))
                    for bp in chains}
        log_keep = {bp: masked(bp[0], log_take[bp] - z[bp]) for bp in chains}
        parts = {bp: _split_bf16(log_keep[bp]) for bp in chains}
        inner = {bp: _dot(parts[bp][0], tri) + _dot(parts[bp][1], tri) for bp in chains}
        runs, accs = list(runs), list(accs)
        weights = {}
        for b, p in chains:
            between = inner[b, p] + _lane_tile(runs[p], SB_KEYS // LANES)
            weights[b, p] = masked(b, jnp.exp(log_take[b, p] + between)).astype(BF16)
            runs[p] = runs[p] + jnp.sum(log_keep[b, p], axis=-1, keepdims=True)
        for b, p in chains:
            accs[p] = accs[p] + _dot(weights[b, p], vts[b])
        return runs, accs

    def any_alive(runs):
        return jnp.max(functools.reduce(jnp.maximum, runs)) > SB_DEAD

    zeros = [jnp.zeros((SB_KEYS, LANES), F32)] * SB_PAIR
    runs, accs = add_blocks([(qi, True, None), (jnp.maximum(qi - 1, 0), False, qi > 0)], zeros, zeros)
    for p in range(SB_PAIR):
        run_ref[p], acc_ref[p] = runs[p], accs[p]

    def more(carry):
        jb, _ = carry
        runs, accs = add_blocks([(jb, False, None)], [run_ref[p] for p in range(SB_PAIR)],
                                [acc_ref[p] for p in range(SB_PAIR)])
        for p in range(SB_PAIR):
            run_ref[p], acc_ref[p] = runs[p], accs[p]
        return jb - 1, any_alive(runs)

    lax.while_loop(lambda c: (c[0] >= 0) & c[1], more, (qi - 2, any_alive(runs)))
    out = acc_ref[0]
    for p in range(1, SB_PAIR):
        out = jnp.where((lane // SB_DIM) == p, acc_ref[p], out)
    o_ref[0] = out.astype(o_ref.dtype)


def _stickbreak(h3):
    bsz, seq, _ = h3.shape
    assert seq % SB_KEYS == 0
    n_pairs = SB_W // LANES
    base = 4 * RET_W // LANES
    idx = np.arange(SB_KEYS)
    tri = jnp.asarray((idx[:, None] > idx[None, :]).astype(np.float32), BF16)
    return pl.pallas_call(
        _stickbreak_kernel,
        out_shape=jax.ShapeDtypeStruct((bsz, seq, SB_W), BF16),
        grid=(bsz, n_pairs, seq // SB_KEYS),
        in_specs=[
            pl.BlockSpec((1, SB_KEYS, LANES), lambda b, p, i: (b, i, base + p)),
            pl.BlockSpec((1, seq, LANES), lambda b, p, i: (b, 0, base + n_pairs + p)),
            pl.BlockSpec((1, seq, LANES), lambda b, p, i: (b, 0, base + 2 * n_pairs + p)),
            _resident((SB_KEYS, SB_KEYS)),
        ],
        out_specs=pl.BlockSpec((1, SB_KEYS, LANES), lambda b, p, i: (b, i, p)),
        scratch_shapes=[pltpu.VMEM((SB_PAIR, SB_KEYS, LANES), F32),
                        pltpu.VMEM((SB_PAIR, SB_KEYS, LANES), F32)],
        compiler_params=_params(("parallel", "parallel", "arbitrary")),
        name="stickbreak",
    )(h3, h3, h3, tri)


DA_TQ = 512
DA_TK = 512
DA_QB = DA_TQ // BLOCK
DA_KB = DA_TK // BLOCK


def _bucket_tiles():
    rel = np.arange(2 * BLOCK)
    max_exact = REL_BUCKETS // 2
    nf = np.maximum(rel, 1).astype(np.float32)
    large = max_exact + (np.log(nf / np.float32(max_exact)) / np.float32(math.log(REL_MAX_DIST / max_exact))
                         * np.float32(REL_BUCKETS - max_exact)).astype(np.int32)
    bucket = np.where(rel < max_exact, rel, np.minimum(large, REL_BUCKETS - 1)).astype(np.int32)
    assert bucket[BLOCK - 1:].min() == REL_BUCKETS - 1
    idx = np.arange(BLOCK)
    d = idx[:, None] - idx[None, :]
    return np.stack([bucket[np.maximum(d, 0)], bucket[d + BLOCK]]).astype(np.int32)


def _diffattn_kernel(tab_ref, q_ref, k_ref, v_ref, bidx_ref, lam_ref, subg_ref, o_ref,
                     bias_ref, vext_ref, acc_ref, m_ref, s_ref, *, lam_init):
    head, b, qi = pl.program_id(0), pl.program_id(1), pl.program_id(2)
    far = REL_BUCKETS - 1

    @pl.when((b == 0) & (qi == 0))
    def _():
        row = lax.broadcasted_iota(jnp.int32, (BLOCK, BLOCK), 0)
        col = lax.broadcasted_iota(jnp.int32, (BLOCK, BLOCK), 1)
        bias_ref[0] = jnp.full((BLOCK, BLOCK), NEG_BIG, F32)
        for t in range(2):
            idx = bidx_ref[t]
            bias = jnp.zeros((BLOCK, BLOCK), F32)
            for bk in range(REL_BUCKETS):
                bias = jnp.where(idx == bk, tab_ref[bk, head] - tab_ref[far, head], bias)
            if t == 0:
                bias = jnp.where(col <= row, bias, NEG_BIG)
            bias_ref[1 + t] = bias
        bias_ref[3] = jnp.zeros((BLOCK, BLOCK), F32)

    @pl.when(qi == 0)
    def _():
        vext_ref[:, :LANES] = v_ref[0]
        vext_ref[:, LANES:] = jnp.ones((vext_ref.shape[0], LANES), vext_ref.dtype)

    q = q_ref[0] * (DIFF_DIM ** -0.5)
    lane = lax.broadcasted_iota(jnp.int32, q.shape, 1)
    q_maps = [jnp.where((lane // DIFF_DIM) == p, q, jnp.zeros_like(q)) for p in range(2)]
    acc_ref[...] = jnp.zeros_like(acc_ref)
    m_ref[...] = jnp.full(m_ref.shape, NEG_BIG, F32)

    def scores(j):
        start = pl.multiple_of(j * DA_TK, DA_TK)
        kt = k_ref[0, pl.ds(start, DA_TK), :]
        for p in range(2):
            s_ref[p] = _dot_nt(q_maps[p], kt)

    def absorb(j, near):
        start = pl.multiple_of(j * DA_TK, DA_TK)
        vt = vext_ref[pl.ds(start, DA_TK), :]
        if near:
            rows = []
            for rb in range(DA_QB):
                dist = (qi * DA_QB + rb) - j * DA_KB
                rows.append(jnp.concatenate(
                    [bias_ref[jnp.clip(dist - cb, -1, 2) + 1] for cb in range(DA_KB)], axis=1))
            bias = jnp.concatenate(rows, axis=0)
        maps = range(2)
        s = [s_ref[p] + bias if near else s_ref[p] for p in maps]
        m_old = [m_ref[p] for p in maps]
        m_new = [jnp.maximum(m_old[p], jnp.max(s[p], axis=-1, keepdims=True)) for p in maps]
        e = [jnp.exp(s[p] - _lane_tile(m_new[p], DA_KB)).astype(BF16) for p in maps]
        for p in maps:
            m_ref[p] = m_new[p]
            acc_ref[p] = acc_ref[p] * _lane_tile(jnp.exp(m_old[p] - m_new[p]), 2) + _dot(e[p], vt)

    def step(j, carry, near):
        absorb(j, near)
        scores(j + 1)
        return carry

    n_tiles = (qi * DA_TQ + DA_TQ - 1) // DA_TK + 1
    n_far = jnp.maximum((qi * DA_QB - 1) // DA_KB, 0)
    scores(0)
    lax.fori_loop(0, n_far, functools.partial(step, near=False), 0)
    lax.fori_loop(n_far, n_tiles - 1, functools.partial(step, near=True), 0)
    absorb(n_tiles - 1, True)

    lam_v = lam_ref[...]
    lam = (jnp.exp(jnp.sum(lam_v[0:1] * lam_v[1:2], axis=-1, keepdims=True))
           - jnp.exp(jnp.sum(lam_v[2:3] * lam_v[3:4], axis=-1, keepdims=True)) + lam_init)
    o0 = acc_ref[0, :, :LANES] / acc_ref[0, :, LANES:]
    o1 = acc_ref[1, :, :LANES] / acc_ref[1, :, LANES:]
    o = o0 - lam * o1
    o = o * lax.rsqrt(jnp.mean(o * o, axis=-1, keepdims=True) + LN_EPS) * subg_ref[...]
    o_ref[0] = (o * (1.0 - lam_init)).astype(o_ref.dtype)


def _diffattn(h3, rel_bias, lam_params, sub_g, lam_init):
    bsz, seq, _ = h3.shape
    assert seq % DA_TK == 0 and DA_TK % DA_TQ == 0
    return pl.pallas_call(
        functools.partial(_diffattn_kernel, lam_init=lam_init),
        out_shape=jax.ShapeDtypeStruct((bsz, seq, DIFF_W), BF16),
        grid=(DIFF_HEADS, bsz, seq // DA_TQ),
        in_specs=[
            pl.BlockSpec(memory_space=pltpu.SMEM),
            pl.BlockSpec((1, DA_TQ, LANES), lambda h, b, i: (b, i, h)),
            pl.BlockSpec((1, seq, LANES), lambda h, b, i: (b, 0, DIFF_HEADS + h)),
            pl.BlockSpec((1, seq, LANES), lambda h, b, i: (b, 0, 2 * DIFF_HEADS + h)),
            _resident((2, BLOCK, BLOCK)),
            _resident((4, DIFF_DIM)),
            _resident((1, 2 * DIFF_DIM)),
        ],
        out_specs=pl.BlockSpec((1, DA_TQ, LANES), lambda h, b, i: (b, i, h)),
        scratch_shapes=[pltpu.VMEM((4, BLOCK, BLOCK), F32),
                        pltpu.VMEM((seq, 2 * LANES), BF16),
                        pltpu.VMEM((2, DA_TQ, 2 * LANES), F32),
                        pltpu.VMEM((2, DA_TQ, LANES), F32),
                        pltpu.VMEM((2, DA_TQ, DA_TK), F32)],
        compiler_params=_params(("arbitrary", "arbitrary", "arbitrary")),
        name="diffattn",
    )(rel_bias, h3, h3, h3, jnp.asarray(_bucket_tiles()), lam_params, sub_g)


def _proj_ln_kernel(*refs, n_in):
    a_refs, w_refs = refs[:n_in], refs[n_in:2 * n_in]
    x_ref, g_ref, b_ref, o_ref, obf_ref = refs[2 * n_in:]
    y = ALPHA * x_ref[...]
    for a_ref, w_ref in zip(a_refs, w_refs):
        y = y + _dot(a_ref[...], w_ref[...])
    mu = jnp.mean(y, axis=-1, keepdims=True)
    cen = y - mu
    var = jnp.mean(cen * cen, axis=-1, keepdims=True)
    out = cen * lax.rsqrt(var + LN_EPS) * g_ref[...] + b_ref[...]
    o_ref[...] = out
    obf_ref[...] = out.astype(obf_ref.dtype)


def _proj_ln(acts, weights, x, gain, bias, tm=ROW_TILE):
    m, d = x.shape
    assert m % tm == 0
    n_in = len(acts)
    in_specs = [pl.BlockSpec((tm, a.shape[1]), lambda i: (i, 0)) for a in acts]
    in_specs += [_resident(w.shape) for w in weights]
    in_specs += [pl.BlockSpec((tm, d), lambda i: (i, 0)), _resident((1, d)), _resident((1, d))]
    return pl.pallas_call(
        functools.partial(_proj_ln_kernel, n_in=n_in),
        out_shape=(jax.ShapeDtypeStruct((m, d), F32), jax.ShapeDtypeStruct((m, d), BF16)),
        grid=(m // tm,),
        in_specs=in_specs,
        out_specs=(pl.BlockSpec((tm, d), lambda i: (i, 0)), pl.BlockSpec((tm, d), lambda i: (i, 0))),
        compiler_params=_params(("parallel",)),
        name="proj_ln",
    )(*acts, *weights, x, gain.reshape(1, d), bias.reshape(1, d))


def _shift_rows(h, prev, n):
    tm, width = h.shape
    rot = pltpu.roll(h.reshape(tm // SUBLANES, SUBLANES, width), n, axis=1)
    before = jnp.concatenate([pltpu.roll(prev, n, axis=0)[None], rot[:-1]], axis=0)
    sub = lax.broadcasted_iota(jnp.int32, rot.shape, 1)
    return jnp.where(sub < n, before, rot).reshape(tm, width)


def _ffn_up_kernel(x_ref, w_ref, cw_ref, cb_ref, o_ref, carry_ref, *, tiles_per_seq):
    x = x_ref[...]
    tm = x.shape[0]
    seq_start = (pl.program_id(0) % tiles_per_seq) == 0
    for c in range(D_FF // MXU_N):
        halves = []
        for base in (c * MXU_N, D_FF + c * MXU_N):
            cols = slice(base, base + MXU_N)
            h = _dot(x, w_ref[:, cols])
            prev = jnp.where(seq_start, 0.0, carry_ref[:, cols])
            carry_ref[:, cols] = h[tm - SUBLANES:]
            cw = cw_ref[:, cols]
            halves.append(_shift_rows(h, prev, 2) * cw[0:1] + _shift_rows(h, prev, 1) * cw[1:2]
                          + h * cw[2:3] + cb_ref[:, cols])
        u, g = halves
        o_ref[:, c * MXU_N:(c + 1) * MXU_N] = (g * jax.nn.sigmoid(g) * u).astype(o_ref.dtype)


def _ffn_up(x_bf, w_up, conv_w, conv_b, seq, tm=ROW_TILE):
    m, d = x_bf.shape
    assert seq % tm == 0
    return pl.pallas_call(
        functools.partial(_ffn_up_kernel, tiles_per_seq=seq // tm),
        out_shape=jax.ShapeDtypeStruct((m, D_FF), BF16),
        grid=(m // tm,),
        in_specs=[pl.BlockSpec((tm, d), lambda i: (i, 0)), _resident((d, 2 * D_FF)),
                  _resident((CONV_W, 2 * D_FF)), _resident((1, 2 * D_FF))],
        out_specs=pl.BlockSpec((tm, D_FF), lambda i: (i, 0)),
        scratch_shapes=[pltpu.VMEM((SUBLANES, 2 * D_FF), F32)],
        compiler_params=_params(("arbitrary",)),
        name="ffn_up",
    )(x_bf, w_up, conv_w, conv_b.reshape(1, 2 * D_FF))


def _rotary_tables(seq):
    inv = ROPE_BASE ** (-jnp.arange(0, RET_DIM, 2, dtype=F32) / RET_DIM)
    ang = jnp.arange(seq, dtype=F32)[:, None] * inv[None, :]
    cos, sin = jnp.cos(ang), jnp.sin(ang)
    return jnp.concatenate([cos, cos], -1), jnp.concatenate([-sin, sin], -1)


def kernel(x, w_in_even, w_out_even, w_in_odd, w_out_odd, lam_q1, lam_k1, lam_q2, lam_k2, subln_g,
           rel_bias, w_up, conv_w, conv_b, w_down, ln1_g, ln1_b, ln2_g, ln2_b):
    bsz, seq, d = x.shape
    m = bsz * seq
    cos_t, sin_t = _rotary_tables(seq)
    xf = x.reshape(m, d)
    xb = xf.astype(BF16)
    for l in range(DEPTH):
        i = l // 2
        if l % 2 == 0:
            h = _in_proj(xb, w_in_even[i].astype(BF16), cos_t, sin_t, seq, rotary_cols=2 * RET_W)
            h3 = h.reshape(bsz, seq, EVEN_IN)
            ret = _retention(h3).reshape(m, RET_W)
            sb = _stickbreak(h3).reshape(m, SB_W)
            w_o = w_out_even[i].astype(BF16)
            xf, xb = _proj_ln([ret, sb], [w_o[:RET_W], w_o[RET_W:]], xf, ln1_g[l], ln1_b[l])
        else:
            h = _in_proj(xb, w_in_odd[i].astype(BF16), cos_t, sin_t, seq, rotary_cols=0)
            lam_params = jnp.stack([lam_q1[i], lam_k1[i], lam_q2[i], lam_k2[i]]).astype(F32)
            lam_init = 0.8 - 0.6 * math.exp(-0.3 * l)
            o = _diffattn(h.reshape(bsz, seq, ODD_IN), rel_bias.astype(F32), lam_params,
                          subln_g[i].reshape(1, 2 * DIFF_DIM).astype(F32), lam_init)
            xf, xb = _proj_ln([o.reshape(m, DIFF_W)], [w_out_odd[i].astype(BF16)], xf, ln1_g[l], ln1_b[l])
        f = _ffn_up(xb, w_up[l].astype(BF16), conv_w[l], conv_b[l], seq)
        xf, xb = _proj_ln([f], [w_down[l].astype(BF16)], xf, ln2_g[l], ln2_b[l])
    return xf.reshape(bsz, seq, d)
```

```python
import functools
import math

import numpy as np
import jax
import jax.numpy as jnp
from jax import lax
from jax.experimental import pallas as pl
from jax.experimental.pallas import tpu as pltpu

D_MODEL = 1024
DEPTH = 4
BLOCK = 128
RET_HEADS = 4
RET_DIM = 128
SB_HEADS = 8
SB_DIM = 64
DIFF_HEADS = 8
DIFF_DIM = 64
REL_BUCKETS = 32
REL_MAX_DIST = 128
D_FF = 2816
CONV_W = 3
ALPHA = (2 * DEPTH) ** 0.25
LN_EPS = 1e-5
ROPE_BASE = 10000.0

RET_W = RET_HEADS * RET_DIM
SB_W = SB_HEADS * SB_DIM
EVEN_IN = 4 * RET_W + 3 * SB_W
DIFF_W = DIFF_HEADS * 2 * DIFF_DIM
ODD_IN = 3 * DIFF_W

LANES = 128
SUBLANES = 8
MXU_N = 256
VMEM_LIMIT = 56 * 1024 * 1024
NEG_BIG = -1e30
ROW_TILE = 512

F32 = jnp.float32
BF16 = jnp.bfloat16


def _params(semantics):
    return pltpu.CompilerParams(dimension_semantics=semantics, vmem_limit_bytes=VMEM_LIMIT)


def _resident(shape):
    return pl.BlockSpec(shape, lambda *_: (0,) * len(shape), pipeline_mode=pl.Buffered(1))


def _dot(a, b):
    return jnp.dot(a, b, preferred_element_type=F32)


def _dot_nt(a, b):
    return lax.dot_general(a, b, (((1,), (1,)), ((), ())), preferred_element_type=F32)


def _dot_tn(a, b):
    return lax.dot_general(a, b, (((0,), (0,)), ((), ())), preferred_element_type=F32)


def _lane_tile(x, reps):
    return x if reps == 1 else jnp.concatenate([x] * reps, axis=1)


def _stack_heads(q, width):
    lane = lax.broadcasted_iota(jnp.int32, q.shape, 1)
    zero = jnp.zeros_like(q)
    return jnp.concatenate([jnp.where((lane // width) == p, q, zero) for p in range(LANES // width)], axis=0)


IN_CHUNK = 512


def _in_proj_kernel(x_ref, w_ref, cos_ref, sin_ref, o_ref, *, n_out, rotary_cols):
    x = x_ref[...].astype(BF16)
    for c in range(n_out // IN_CHUNK):
        lo = c * IN_CHUNK
        acc = _dot(x, w_ref[:, lo:lo + IN_CHUNK])
        if lo < rotary_cols:
            cosv, sinv = cos_ref[...], sin_ref[...]
            for hh in range(IN_CHUNK // RET_DIM):
                blk = acc[:, hh * RET_DIM:(hh + 1) * RET_DIM]
                rot = blk * cosv + pltpu.roll(blk, RET_DIM // 2, axis=1) * sinv
                o_ref[:, lo + hh * RET_DIM:lo + (hh + 1) * RET_DIM] = rot.astype(o_ref.dtype)
        else:
            o_ref[:, lo:lo + IN_CHUNK] = acc.astype(o_ref.dtype)


def _in_proj(x_bf, w, cos_t, sin_t, seq, rotary_cols, tm=ROW_TILE):
    m, k = x_bf.shape
    n_out = w.shape[1]
    assert seq % tm == 0 and n_out % IN_CHUNK == 0
    s_tiles = seq // tm
    return pl.pallas_call(
        functools.partial(_in_proj_kernel, n_out=n_out, rotary_cols=rotary_cols),
        out_shape=jax.ShapeDtypeStruct((m, n_out), BF16),
        grid=(m // tm,),
        in_specs=[
            pl.BlockSpec((tm, k), lambda i: (i, 0)),
            _resident((k, n_out)),
            pl.BlockSpec((tm, RET_DIM), lambda i: (i % s_tiles, 0)),
            pl.BlockSpec((tm, RET_DIM), lambda i: (i % s_tiles, 0)),
        ],
        out_specs=pl.BlockSpec((tm, n_out), lambda i: (i, 0)),
        compiler_params=_params(("parallel",)),
        name="in_proj",
    )(x_bf, w, cos_t, sin_t)


RET_CHUNK = 256


def _retention_tables():
    hh = np.arange(RET_HEADS, dtype=np.float32)
    log_g = np.log(np.float32(1.0) - np.float32(2.0) ** (np.float32(-5.0) - hh)).astype(np.float32)
    idx = np.arange(RET_CHUNK, dtype=np.float32)
    rel = idx[:, None] - idx[None, :]
    scale = np.float32(RET_DIM ** -0.5)
    decay = np.where(rel >= 0, np.exp(log_g[:, None, None] * np.maximum(rel, 0.0)), 0.0) * scale
    q_dec = np.exp(log_g[:, None] * (idx[None, :] + 1.0))
    k_dec = np.exp(log_g[:, None] * (RET_CHUNK - 1.0 - idx[None, :])) * scale
    chunk_g = np.exp(log_g * RET_CHUNK)
    bcast = lambda t: np.broadcast_to(t[:, :, None], (RET_HEADS, RET_CHUNK, LANES))
    return (decay.astype(np.float32), np.ascontiguousarray(bcast(q_dec), np.float32),
            np.ascontiguousarray(bcast(k_dec), np.float32), [float(g) for g in chunk_g])


def _retention_kernel(q_ref, k_ref, v_ref, g_ref, decay_ref, qdec_ref, kdec_ref, o_ref, state_ref,
                      *, chunk_g):
    @pl.when(pl.program_id(1) == 0)
    def _():
        state_ref[...] = jnp.zeros_like(state_ref)

    for hh in range(RET_HEADS):
        cols = slice(hh * RET_DIM, (hh + 1) * RET_DIM)
        q, k, v = q_ref[0, :, cols], k_ref[0, :, cols], v_ref[0, :, cols]
        state = state_ref[hh]
        s = _dot_nt(q, k) * decay_ref[hh]
        out = _dot(s.astype(BF16), v) + _dot(q, state.astype(BF16)) * qdec_ref[hh]
        k_scaled = (k.astype(F32) * kdec_ref[hh]).astype(BF16)
        state_ref[hh] = state * chunk_g[hh] + _dot_tn(k_scaled, v)
        mu = jnp.mean(out, axis=-1, keepdims=True)
        cen = out - mu
        var = jnp.mean(cen * cen, axis=-1, keepdims=True)
        gate = g_ref[0, :, cols].astype(F32)
        o_ref[0, :, cols] = (cen * lax.rsqrt(var + LN_EPS) * (gate * jax.nn.sigmoid(gate))).astype(o_ref.dtype)


def _retention(h3):
    bsz, seq, _ = h3.shape
    decay, q_dec, k_dec, chunk_g = _retention_tables()
    assert seq % RET_CHUNK == 0
    col = lambda j: pl.BlockSpec((1, RET_CHUNK, RET_W), lambda b, c, j=j: (b, c, j))
    return pl.pallas_call(
        functools.partial(_retention_kernel, chunk_g=chunk_g),
        out_shape=jax.ShapeDtypeStruct((bsz, seq, RET_W), BF16),
        grid=(bsz, seq // RET_CHUNK),
        in_specs=[col(0), col(1), col(2), col(3),
                  _resident((RET_HEADS, RET_CHUNK, RET_CHUNK)), _resident((RET_HEADS, RET_CHUNK, LANES)),
                  _resident((RET_HEADS, RET_CHUNK, LANES))],
        out_specs=pl.BlockSpec((1, RET_CHUNK, RET_W), lambda b, c: (b, c, 0)),
        scratch_shapes=[pltpu.VMEM((RET_HEADS, RET_DIM, RET_DIM), F32)],
        compiler_params=_params(("parallel", "arbitrary")),
        name="retention",
    )(h3, h3, h3, h3, jnp.asarray(decay), jnp.asarray(q_dec), jnp.asarray(k_dec))


SB_PAIR = LANES // SB_DIM
SB_KEYS = MXU_N
SB_DEAD = -104.0


def _split_bf16(x):
    hi = x.astype(BF16)
    lo = (x - hi.astype(F32)).astype(BF16)
    return hi, lo


def _stickbreak_kernel(q_ref, k_ref, v_ref, tri_ref, o_ref, acc_ref, run_ref):
    qi = pl.program_id(2)
    q = q_ref[0] * (SB_DIM ** -0.5)
    lane = lax.broadcasted_iota(jnp.int32, q.shape, 1)
    q_heads = [jnp.where((lane // SB_DIM) == p, q, jnp.zeros_like(q)) for p in range(SB_PAIR)]
    tri = tri_ref[...]

    r = lax.broadcasted_iota(jnp.int32, (SB_KEYS, SB_KEYS), 0)
    c = lax.broadcasted_iota(jnp.int32, (SB_KEYS, SB_KEYS), 1)
    past = c < r

    def add_blocks(blocks, runs, accs):
        kts, vts = [], []
        for jb, _, _ in blocks:
            start = pl.multiple_of(jb * SB_KEYS, SB_KEYS)
            kts.append(k_ref[0, pl.ds(start, SB_KEYS), :])
            vts.append(v_ref[0, pl.ds(start, SB_KEYS), :])
        chains = [(b, p) for b in range(len(blocks)) for p in range(SB_PAIR)]

        def masked(b, x):
            _, diagonal, valid = blocks[b]
            if diagonal:
                x = jnp.where(past, x, 0.0)
            if valid is not None:
                x = jnp.where(valid, x, 0.0)
            return x

        z = {bp: _dot_nt(q_heads[bp[1]], kts[bp[0]]) for bp in chains}
        log_take = {bp: jnp.minimum(z[bp], 0.0) - jnp.log(1.0 + jnp.exp(-jnp.abs(z[bp])))
                    for bp in chains}
        log_keep = {bp: masked(bp[0], log_take[bp] - z[bp]) for bp in chains}
        parts = {bp: _split_bf16(log_keep[bp]) for bp in chains}
        inner = {bp: _dot(parts[bp][0], tri) + _dot(parts[bp][1], tri) for bp in chains}
        runs, accs = list(runs), list(accs)
        weights = {}
        for b, p in chains:
            between = inner[b, p] + _lane_tile(runs[p], SB_KEYS // LANES)
            weights[b, p] = masked(b, jnp.exp(log_take[b, p] + between)).astype(BF16)
            runs[p] = runs[p] + jnp.sum(log_keep[b, p], axis=-1, keepdims=True)
        for b, p in chains:
            accs[p] = accs[p] + _dot(weights[b, p], vts[b])
        return runs, accs

    def any_alive(runs):
        return jnp.max(functools.reduce(jnp.maximum, runs)) > SB_DEAD

    zeros = [jnp.zeros((SB_KEYS, LANES), F32)] * SB_PAIR
    runs, accs = add_blocks([(qi, True, None), (jnp.maximum(qi - 1, 0), False, qi > 0)], zeros, zeros)
    for p in range(SB_PAIR):
        run_ref[p], acc_ref[p] = runs[p], accs[p]

    def more(carry):
        jb, _ = carry
        runs, accs = add_blocks([(jb, False, None)], [run_ref[p] for p in range(SB_PAIR)],
                                [acc_ref[p] for p in range(SB_PAIR)])
        for p in range(SB_PAIR):
            run_ref[p], acc_ref[p] = runs[p], accs[p]
        return jb - 1, any_alive(runs)

    lax.while_loop(lambda c: (c[0] >= 0) & c[1], more, (qi - 2, any_alive(runs)))
    out = acc_ref[0]
    for p in range(1, SB_PAIR):
        out = jnp.where((lane // SB_DIM) == p, acc_ref[p], out)
    o_ref[0] = out.astype(o_ref.dtype)


def _stickbreak(h3):
    bsz, seq, _ = h3.shape
    assert seq % SB_KEYS == 0
    n_pairs = SB_W // LANES
    base = 4 * RET_W // LANES
    idx = np.arange(SB_KEYS)
    tri = jnp.asarray((idx[:, None] > idx[None, :]).astype(np.float32), BF16)
    return pl.pallas_call(
        _stickbreak_kernel,
        out_shape=jax.ShapeDtypeStruct((bsz, seq, SB_W), BF16),
        grid=(bsz, n_pairs, seq // SB_KEYS),
        in_specs=[
            pl.BlockSpec((1, SB_KEYS, LANES), lambda b, p, i: (b, i, base + p)),
            pl.BlockSpec((1, seq, LANES), lambda b, p, i: (b, 0, base + n_pairs + p)),
            pl.BlockSpec((1, seq, LANES), lambda b, p, i: (b, 0, base + 2 * n_pairs + p)),
            _resident((SB_KEYS, SB_KEYS)),
        ],
        out_specs=pl.BlockSpec((1, SB_KEYS, LANES), lambda b, p, i: (b, i, p)),
        scratch_shapes=[pltpu.VMEM((SB_PAIR, SB_KEYS, LANES), F32),
                        pltpu.VMEM((SB_PAIR, SB_KEYS, LANES), F32)],
        compiler_params=_params(("parallel", "parallel", "arbitrary")),
        name="stickbreak",
    )(h3, h3, h3, tri)


DA_TQ = 512
DA_TK = 512
DA_QB = DA_TQ // BLOCK
DA_KB = DA_TK // BLOCK


def _bucket_tiles():
    rel = np.arange(2 * BLOCK)
    max_exact = REL_BUCKETS // 2
    nf = np.maximum(rel, 1).astype(np.float32)
    large = max_exact + (np.log(nf / np.float32(max_exact)) / np.float32(math.log(REL_MAX_DIST / max_exact))
                         * np.float32(REL_BUCKETS - max_exact)).astype(np.int32)
    bucket = np.where(rel < max_exact, rel, np.minimum(large, REL_BUCKETS - 1)).astype(np.int32)
    assert bucket[BLOCK - 1:].min() == REL_BUCKETS - 1
    idx = np.arange(BLOCK)
    d = idx[:, None] - idx[None, :]
    return np.stack([bucket[np.maximum(d, 0)], bucket[d + BLOCK]]).astype(np.int32)


def _diffattn_kernel(tab_ref, q_ref, k_ref, v_ref, bidx_ref, lam_ref, subg_ref, o_ref,
                     bias_ref, vext_ref, acc_ref, m_ref, s_ref, *, lam_init):
    head, b, qi = pl.program_id(0), pl.program_id(1), pl.program_id(2)
    far = REL_BUCKETS - 1

    @pl.when((b == 0) & (qi == 0))
    def _():
        row = lax.broadcasted_iota(jnp.int32, (BLOCK, BLOCK), 0)
        col = lax.broadcasted_iota(jnp.int32, (BLOCK, BLOCK), 1)
        bias_ref[0] = jnp.full((BLOCK, BLOCK), NEG_BIG, F32)
        for t in range(2):
            idx = bidx_ref[t]
            bias = jnp.zeros((BLOCK, BLOCK), F32)
            for bk in range(REL_BUCKETS):
                bias = jnp.where(idx == bk, tab_ref[bk, head] - tab_ref[far, head], bias)
            if t == 0:
                bias = jnp.where(col <= row, bias, NEG_BIG)
            bias_ref[1 + t] = bias
        bias_ref[3] = jnp.zeros((BLOCK, BLOCK), F32)

    @pl.when(qi == 0)
    def _():
        vext_ref[:, :LANES] = v_ref[0]
        vext_ref[:, LANES:] = jnp.ones((vext_ref.shape[0], LANES), vext_ref.dtype)

    q = q_ref[0] * (DIFF_DIM ** -0.5)
    lane = lax.broadcasted_iota(jnp.int32, q.shape, 1)
    q_maps = [jnp.where((lane // DIFF_DIM) == p, q, jnp.zeros_like(q)) for p in range(2)]
    acc_ref[...] = jnp.zeros_like(acc_ref)
    m_ref[...] = jnp.full(m_ref.shape, NEG_BIG, F32)

    def scores(j):
        start = pl.multiple_of(j * DA_TK, DA_TK)
        kt = k_ref[0, pl.ds(start, DA_TK), :]
        for p in range(2):
            s_ref[p] = _dot_nt(q_maps[p], kt)

    def absorb(j, near):
        start = pl.multiple_of(j * DA_TK, DA_TK)
        vt = vext_ref[pl.ds(start, DA_TK), :]
        if near:
            rows = []
            for rb in range(DA_QB):
                dist = (qi * DA_QB + rb) - j * DA_KB
                rows.append(jnp.concatenate(
                    [bias_ref[jnp.clip(dist - cb, -1, 2) + 1] for cb in range(DA_KB)], axis=1))
            bias = jnp.concatenate(rows, axis=0)
        maps = range(2)
        s = [s_ref[p] + bias if near else s_ref[p] for p in maps]
        m_old = [m_ref[p] for p in maps]
        m_new = [jnp.maximum(m_old[p], jnp.max(s[p], axis=-1, keepdims=True)) for p in maps]
        e = [jnp.exp(s[p] - _lane_tile(m_new[p], DA_KB)).astype(BF16) for p in maps]
        for p in maps:
            m_ref[p] = m_new[p]
            acc_ref[p] = acc_ref[p] * _lane_tile(jnp.exp(m_old[p] - m_new[p]), 2) + _dot(e[p], vt)

    def step(j, carry, near):
        absorb(j, near)
        scores(j + 1)
        return carry

    n_tiles = (qi * DA_TQ + DA_TQ - 1) // DA_TK + 1
    n_far = jnp.maximum((qi * DA_QB - 1) // DA_KB, 0)
    scores(0)
    lax.fori_loop(0, n_far, functools.partial(step, near=False), 0)
    lax.fori_loop(n_far, n_tiles - 1, functools.partial(step, near=True), 0)
    absorb(n_tiles - 1, True)

    lam_v = lam_ref[...]
    lam = (jnp.exp(jnp.sum(lam_v[0:1] * lam_v[1:2], axis=-1, keepdims=True))
           - jnp.exp(jnp.sum(lam_v[2:3] * lam_v[3:4], axis=-1, keepdims=True)) + lam_init)
    o0 = acc_ref[0, :, :LANES] / acc_ref[0, :, LANES:]
    o1 = acc_ref[1, :, :LANES] / acc_ref[1, :, LANES:]
    o = o0 - lam * o1
    o = o * lax.rsqrt(jnp.mean(o * o, axis=-1, keepdims=True) + LN_EPS) * subg_ref[...]
    o_ref[0] = (o * (1.0 - lam_init)).astype(o_ref.dtype)


def _diffattn(h3, rel_bias, lam_params, sub_g, lam_init):
    bsz, seq, _ = h3.shape
    assert seq % DA_TK == 0 and DA_TK % DA_TQ == 0
    return pl.pallas_call(
        functools.partial(_diffattn_kernel, lam_init=lam_init),
        out_shape=jax.ShapeDtypeStruct((bsz, seq, DIFF_W), BF16),
        grid=(DIFF_HEADS, bsz, seq // DA_TQ),
        in_specs=[
            pl.BlockSpec(memory_space=pltpu.SMEM),
            pl.BlockSpec((1, DA_TQ, LANES), lambda h, b, i: (b, i, h)),
            pl.BlockSpec((1, seq, LANES), lambda h, b, i: (b, 0, DIFF_HEADS + h)),
            pl.BlockSpec((1, seq, LANES), lambda h, b, i: (b, 0, 2 * DIFF_HEADS + h)),
            _resident((2, BLOCK, BLOCK)),
            _resident((4, DIFF_DIM)),
            _resident((1, 2 * DIFF_DIM)),
        ],
        out_specs=pl.BlockSpec((1, DA_TQ, LANES), lambda h, b, i: (b, i, h)),
        scratch_shapes=[pltpu.VMEM((4, BLOCK, BLOCK), F32),
                        pltpu.VMEM((seq, 2 * LANES), BF16),
                        pltpu.VMEM((2, DA_TQ, 2 * LANES), F32),
                        pltpu.VMEM((2, DA_TQ, LANES), F32),
                        pltpu.VMEM((2, DA_TQ, DA_TK), F32)],
        compiler_params=_params(("arbitrary", "arbitrary", "arbitrary")),
        name="diffattn",
    )(rel_bias, h3, h3, h3, jnp.asarray(_bucket_tiles()), lam_params, sub_g)


def _proj_ln_kernel(*refs, n_in):
    a_refs, w_refs = refs[:n_in], refs[n_in:2 * n_in]
    x_ref, g_ref, b_ref, o_ref, obf_ref = refs[2 * n_in:]
    y = ALPHA * x_ref[...]
    for a_ref, w_ref in zip(a_refs, w_refs):
        y = y + _dot(a_ref[...], w_ref[...])
    mu = jnp.mean(y, axis=-1, keepdims=True)
    cen = y - mu
    var = jnp.mean(cen * cen, axis=-1, keepdims=True)
    out = cen * lax.rsqrt(var + LN_EPS) * g_ref[...] + b_ref[...]
    o_ref[...] = out
    obf_ref[...] = out.astype(obf_ref.dtype)


def _proj_ln(acts, weights, x, gain, bias, tm=ROW_TILE):
    m, d = x.shape
    assert m % tm == 0
    n_in = len(acts)
    in_specs = [pl.BlockSpec((tm, a.shape[1]), lambda i: (i, 0)) for a in acts]
    in_specs += [_resident(w.shape) for w in weights]
    in_specs += [pl.BlockSpec((tm, d), lambda i: (i, 0)), _resident((1, d)), _resident((1, d))]
    return pl.pallas_call(
        functools.partial(_proj_ln_kernel, n_in=n_in),
        out_shape=(jax.ShapeDtypeStruct((m, d), F32), jax.ShapeDtypeStruct((m, d), BF16)),
        grid=(m // tm,),
        in_specs=in_specs,
        out_specs=(pl.BlockSpec((tm, d), lambda i: (i, 0)), pl.BlockSpec((tm, d), lambda i: (i, 0))),
        compiler_params=_params(("parallel",)),
        name="proj_ln",
    )(*acts, *weights, x, gain.reshape(1, d), bias.reshape(1, d))


def _shift_rows(h, prev, n):
    row = lax.broadcasted_iota(jnp.int32, h.shape, 0)
    out = pltpu.roll(h, n, axis=0)
    for r in range(n):
        out = jnp.where(row == r, prev[SUBLANES - n + r:SUBLANES - n + r + 1], out)
    return out


def _ffn_up_kernel(x_ref, w_ref, cw_ref, cb_ref, o_ref, carry_ref, *, tiles_per_seq):
    x = x_ref[...]
    tm = x.shape[0]
    seq_start = (pl.program_id(0) % tiles_per_seq) == 0
    for c in range(D_FF // MXU_N):
        halves = []
        for base in (c * MXU_N, D_FF + c * MXU_N):
            cols = slice(base, base + MXU_N)
            h = _dot(x, w_ref[:, cols])
            prev = jnp.where(seq_start, 0.0, carry_ref[:, cols])
            carry_ref[:, cols] = h[tm - SUBLANES:]
            cw = cw_ref[:, cols]
            halves.append(_shift_rows(h, prev, 2) * cw[0:1] + _shift_rows(h, prev, 1) * cw[1:2]
                          + h * cw[2:3] + cb_ref[:, cols])
        u, g = halves
        o_ref[:, c * MXU_N:(c + 1) * MXU_N] = (g * jax.nn.sigmoid(g) * u).astype(o_ref.dtype)


def _ffn_up(x_bf, w_up, conv_w, conv_b, seq, tm=ROW_TILE):
    m, d = x_bf.shape
    assert seq % tm == 0
    return pl.pallas_call(
        functools.partial(_ffn_up_kernel, tiles_per_seq=seq // tm),
        out_shape=jax.ShapeDtypeStruct((m, D_FF), BF16),
        grid=(m // tm,),
        in_specs=[pl.BlockSpec((tm, d), lambda i: (i, 0)), _resident((d, 2 * D_FF)),
                  _resident((CONV_W, 2 * D_FF)), _resident((1, 2 * D_FF))],
        out_specs=pl.BlockSpec((tm, D_FF), lambda i: (i, 0)),
        scratch_shapes=[pltpu.VMEM((SUBLANES, 2 * D_FF), F32)],
        compiler_params=_params(("arbitrary",)),
        name="ffn_up",
    )(x_bf, w_up, conv_w, conv_b.reshape(1, 2 * D_FF))


def _rotary_tables(seq):
    inv = ROPE_BASE ** (-jnp.arange(0, RET_DIM, 2, dtype=F32) / RET_DIM)
    ang = jnp.arange(seq, dtype=F32)[:, None] * inv[None, :]
    cos, sin = jnp.cos(ang), jnp.sin(ang)
    return jnp.concatenate([cos, cos], -1), jnp.concatenate([-sin, sin], -1)


def kernel(x, w_in_even, w_out_even, w_in_odd, w_out_odd, lam_q1, lam_k1, lam_q2, lam_k2, subln_g,
           rel_bias, w_up, conv_w, conv_b, w_down, ln1_g, ln1_b, ln2_g, ln2_b):
    bsz, seq, d = x.shape
    m = bsz * seq
    cos_t, sin_t = _rotary_tables(seq)
    xf = x.reshape(m, d)
    xb = xf
    for l in range(DEPTH):
        i = l // 2
        if l % 2 == 0:
            h = _in_proj(xb, w_in_even[i].astype(BF16), cos_t, sin_t, seq, rotary_cols=2 * RET_W)
            h3 = h.reshape(bsz, seq, EVEN_IN)
            ret = _retention(h3).reshape(m, RET_W)
            sb = _stickbreak(h3).reshape(m, SB_W)
            w_o = w_out_even[i].astype(BF16)
            xf, xb = _proj_ln([ret, sb], [w_o[:RET_W], w_o[RET_W:]], xf, ln1_g[l], ln1_b[l])
        else:
            h = _in_proj(xb, w_in_odd[i].astype(BF16), cos_t, sin_t, seq, rotary_cols=0)
            lam_params = jnp.stack([lam_q1[i], lam_k1[i], lam_q2[i], lam_k2[i]]).astype(F32)
            lam_init = 0.8 - 0.6 * math.exp(-0.3 * l)
            o = _diffattn(h.reshape(bsz, seq, ODD_IN), rel_bias.astype(F32), lam_params,
                          subln_g[i].reshape(1, 2 * DIFF_DIM).astype(F32), lam_init)
            xf, xb = _proj_ln([o.reshape(m, DIFF_W)], [w_out_odd[i].astype(BF16)], xf, ln1_g[l], ln1_b[l])
        f = _ffn_up(xb, w_up[l].astype(BF16), conv_w[l], conv_b[l], seq)
        xf, xb = _proj_ln([f], [w_down[l].astype(BF16)], xf, ln2_g[l], ln2_b[l])
    return xf.reshape(bsz, seq, d)
```

```python
import functools
import math

import numpy as np
import jax
import jax.numpy as jnp
from jax import lax
from jax.experimental import pallas as pl
from jax.experimental.pallas import tpu as pltpu

D_MODEL = 1024
DEPTH = 4
BLOCK = 128
RET_HEADS = 4
RET_DIM = 128
SB_HEADS = 8
SB_DIM = 64
DIFF_HEADS = 8
DIFF_DIM = 64
REL_BUCKETS = 32
REL_MAX_DIST = 128
D_FF = 2816
CONV_W = 3
ALPHA = (2 * DEPTH) ** 0.25
LN_EPS = 1e-5
ROPE_BASE = 10000.0

RET_W = RET_HEADS * RET_DIM
SB_W = SB_HEADS * SB_DIM
EVEN_IN = 4 * RET_W + 3 * SB_W
DIFF_W = DIFF_HEADS * 2 * DIFF_DIM
ODD_IN = 3 * DIFF_W

LANES = 128
SUBLANES = 8
MXU_N = 256
VMEM_LIMIT = 56 * 1024 * 1024
NEG_BIG = -1e30
ROW_TILE = 512

F32 = jnp.float32
BF16 = jnp.bfloat16


def _params(semantics):
    return pltpu.CompilerParams(dimension_semantics=semantics, vmem_limit_bytes=VMEM_LIMIT)


def _resident(shape):
    return pl.BlockSpec(shape, lambda *_: (0,) * len(shape), pipeline_mode=pl.Buffered(1))


def _dot(a, b):
    return jnp.dot(a, b, preferred_element_type=F32)


def _dot_nt(a, b):
    return lax.dot_general(a, b, (((1,), (1,)), ((), ())), preferred_element_type=F32)


def _dot_tn(a, b):
    return lax.dot_general(a, b, (((0,), (0,)), ((), ())), preferred_element_type=F32)


def _lane_tile(x, reps):
    return x if reps == 1 else jnp.concatenate([x] * reps, axis=1)


def _stack_heads(q, width):
    lane = lax.broadcasted_iota(jnp.int32, q.shape, 1)
    zero = jnp.zeros_like(q)
    return jnp.concatenate([jnp.where((lane // width) == p, q, zero) for p in range(LANES // width)], axis=0)


IN_CHUNK = 512


def _cast_weight_once(w_ref, wb_ref, chunk):
    @pl.when(pl.program_id(0) == 0)
    def _():
        for lo in range(0, w_ref.shape[1], chunk):
            wb_ref[:, lo:lo + chunk] = w_ref[:, lo:lo + chunk].astype(wb_ref.dtype)


def _in_proj_kernel(x_ref, w_ref, cos_ref, sin_ref, o_ref, wb_ref, *, n_out, rotary_cols):
    _cast_weight_once(w_ref, wb_ref, IN_CHUNK)
    x = x_ref[...].astype(BF16)
    for c in range(n_out // IN_CHUNK):
        lo = c * IN_CHUNK
        acc = _dot(x, wb_ref[:, lo:lo + IN_CHUNK])
        if lo < rotary_cols:
            cosv, sinv = cos_ref[...], sin_ref[...]
            for hh in range(IN_CHUNK // RET_DIM):
                blk = acc[:, hh * RET_DIM:(hh + 1) * RET_DIM]
                rot = blk * cosv + pltpu.roll(blk, RET_DIM // 2, axis=1) * sinv
                o_ref[:, lo + hh * RET_DIM:lo + (hh + 1) * RET_DIM] = rot.astype(o_ref.dtype)
        else:
            o_ref[:, lo:lo + IN_CHUNK] = acc.astype(o_ref.dtype)


def _in_proj(x_bf, w, cos_t, sin_t, seq, rotary_cols, tm=ROW_TILE):
    m, k = x_bf.shape
    n_out = w.shape[1]
    assert seq % tm == 0 and n_out % IN_CHUNK == 0
    s_tiles = seq // tm
    return pl.pallas_call(
        functools.partial(_in_proj_kernel, n_out=n_out, rotary_cols=rotary_cols),
        out_shape=jax.ShapeDtypeStruct((m, n_out), BF16),
        grid=(m // tm,),
        in_specs=[
            pl.BlockSpec((tm, k), lambda i: (i, 0)),
            _resident((k, n_out)),
            pl.BlockSpec((tm, RET_DIM), lambda i: (i % s_tiles, 0)),
            pl.BlockSpec((tm, RET_DIM), lambda i: (i % s_tiles, 0)),
        ],
        out_specs=pl.BlockSpec((tm, n_out), lambda i: (i, 0)),
        scratch_shapes=[pltpu.VMEM((k, n_out), BF16)],
        compiler_params=_params(("arbitrary",)),
        name="in_proj",
    )(x_bf, w, cos_t, sin_t)


RET_CHUNK = 256


def _retention_tables():
    hh = np.arange(RET_HEADS, dtype=np.float32)
    log_g = np.log(np.float32(1.0) - np.float32(2.0) ** (np.float32(-5.0) - hh)).astype(np.float32)
    idx = np.arange(RET_CHUNK, dtype=np.float32)
    rel = idx[:, None] - idx[None, :]
    scale = np.float32(RET_DIM ** -0.5)
    decay = np.where(rel >= 0, np.exp(log_g[:, None, None] * np.maximum(rel, 0.0)), 0.0) * scale
    q_dec = np.exp(log_g[:, None] * (idx[None, :] + 1.0))
    k_dec = np.exp(log_g[:, None] * (RET_CHUNK - 1.0 - idx[None, :])) * scale
    chunk_g = np.exp(log_g * RET_CHUNK)
    bcast = lambda t: np.broadcast_to(t[:, :, None], (RET_HEADS, RET_CHUNK, LANES))
    return (decay.astype(np.float32), np.ascontiguousarray(bcast(q_dec), np.float32),
            np.ascontiguousarray(bcast(k_dec), np.float32), [float(g) for g in chunk_g])


def _retention_kernel(q_ref, k_ref, v_ref, g_ref, decay_ref, qdec_ref, kdec_ref, o_ref, state_ref,
                      *, chunk_g):
    @pl.when(pl.program_id(1) == 0)
    def _():
        state_ref[...] = jnp.zeros_like(state_ref)

    for hh in range(RET_HEADS):
        cols = slice(hh * RET_DIM, (hh + 1) * RET_DIM)
        q, k, v = q_ref[0, :, cols], k_ref[0, :, cols], v_ref[0, :, cols]
        state = state_ref[hh]
        s = _dot_nt(q, k) * decay_ref[hh]
        out = _dot(s.astype(BF16), v) + _dot(q, state.astype(BF16)) * qdec_ref[hh]
        k_scaled = (k.astype(F32) * kdec_ref[hh]).astype(BF16)
        state_ref[hh] = state * chunk_g[hh] + _dot_tn(k_scaled, v)
        mu = jnp.mean(out, axis=-1, keepdims=True)
        cen = out - mu
        var = jnp.mean(cen * cen, axis=-1, keepdims=True)
        gate = g_ref[0, :, cols].astype(F32)
        o_ref[0, :, cols] = (cen * lax.rsqrt(var + LN_EPS) * (gate * jax.nn.sigmoid(gate))).astype(o_ref.dtype)


def _retention(h3):
    bsz, seq, _ = h3.shape
    decay, q_dec, k_dec, chunk_g = _retention_tables()
    assert seq % RET_CHUNK == 0
    col = lambda j: pl.BlockSpec((1, RET_CHUNK, RET_W), lambda b, c, j=j: (b, c, j))
    return pl.pallas_call(
        functools.partial(_retention_kernel, chunk_g=chunk_g),
        out_shape=jax.ShapeDtypeStruct((bsz, seq, RET_W), BF16),
        grid=(bsz, seq // RET_CHUNK),
        in_specs=[col(0), col(1), col(2), col(3),
                  _resident((RET_HEADS, RET_CHUNK, RET_CHUNK)), _resident((RET_HEADS, RET_CHUNK, LANES)),
                  _resident((RET_HEADS, RET_CHUNK, LANES))],
        out_specs=pl.BlockSpec((1, RET_CHUNK, RET_W), lambda b, c: (b, c, 0)),
        scratch_shapes=[pltpu.VMEM((RET_HEADS, RET_DIM, RET_DIM), F32)],
        compiler_params=_params(("parallel", "arbitrary")),
        name="retention",
    )(h3, h3, h3, h3, jnp.asarray(decay), jnp.asarray(q_dec), jnp.asarray(k_dec))


SB_PAIR = LANES // SB_DIM
SB_KEYS = MXU_N
SB_DEAD = -104.0


def _split_bf16(x):
    hi = x.astype(BF16)
    lo = (x - hi.astype(F32)).astype(BF16)
    return hi, lo


def _stickbreak_kernel(q_ref, k_ref, v_ref, tri_ref, o_ref, acc_ref, run_ref):
    qi = pl.program_id(2)
    q = q_ref[0] * (SB_DIM ** -0.5)
    lane = lax.broadcasted_iota(jnp.int32, q.shape, 1)
    q_heads = [jnp.where((lane // SB_DIM) == p, q, jnp.zeros_like(q)) for p in range(SB_PAIR)]
    tri = tri_ref[...]

    r = lax.broadcasted_iota(jnp.int32, (SB_KEYS, SB_KEYS), 0)
    c = lax.broadcasted_iota(jnp.int32, (SB_KEYS, SB_KEYS), 1)
    past = c < r

    def add_blocks(blocks, runs, accs):
        kts, vts = [], []
        for jb, _, _ in blocks:
            start = pl.multiple_of(jb * SB_KEYS, SB_KEYS)
            kts.append(k_ref[0, pl.ds(start, SB_KEYS), :])
            vts.append(v_ref[0, pl.ds(start, SB_KEYS), :])
        chains = [(b, p) for b in range(len(blocks)) for p in range(SB_PAIR)]

        def masked(b, x):
            _, diagonal, valid = blocks[b]
            if diagonal:
                x = jnp.where(past, x, 0.0)
            if valid is not None:
                x = jnp.where(valid, x, 0.0)
            return x

        z = {bp: _dot_nt(q_heads[bp[1]], kts[bp[0]]) for bp in chains}
        log_take = {bp: jnp.minimum(z[bp], 0.0) - jnp.log(1.0 + jnp.exp(-jnp.abs(z[bp])))
                    for bp in chains}
        log_keep = {bp: masked(bp[0], log_take[bp] - z[bp]) for bp in chains}
        parts = {bp: _split_bf16(log_keep[bp]) for bp in chains}
        inner = {bp: _dot(parts[bp][0], tri) + _dot(parts[bp][1], tri) for bp in chains}
        runs, accs = list(runs), list(accs)
        weights = {}
        for b, p in chains:
            between = inner[b, p] + _lane_tile(runs[p], SB_KEYS // LANES)
            weights[b, p] = masked(b, jnp.exp(log_take[b, p] + between)).astype(BF16)
            runs[p] = runs[p] + jnp.sum(log_keep[b, p], axis=-1, keepdims=True)
        for b, p in chains:
            accs[p] = accs[p] + _dot(weights[b, p], vts[b])
        return runs, accs

    def any_alive(runs):
        return jnp.max(functools.reduce(jnp.maximum, runs)) > SB_DEAD

    zeros = [jnp.zeros((SB_KEYS, LANES), F32)] * SB_PAIR
    runs, accs = add_blocks([(qi, True, None), (jnp.maximum(qi - 1, 0), False, qi > 0)], zeros, zeros)
    for p in range(SB_PAIR):
        run_ref[p], acc_ref[p] = runs[p], accs[p]

    def more(carry):
        jb, _ = carry
        runs, accs = add_blocks([(jb, False, None)], [run_ref[p] for p in range(SB_PAIR)],
                                [acc_ref[p] for p in range(SB_PAIR)])
        for p in range(SB_PAIR):
            run_ref[p], acc_ref[p] = runs[p], accs[p]
        return jb - 1, any_alive(runs)

    lax.while_loop(lambda c: (c[0] >= 0) & c[1], more, (qi - 2, any_alive(runs)))
    out = acc_ref[0]
    for p in range(1, SB_PAIR):
        out = jnp.where((lane // SB_DIM) == p, acc_ref[p], out)
    o_ref[0] = out.astype(o_ref.dtype)


def _stickbreak(h3):
    bsz, seq, _ = h3.shape
    assert seq % SB_KEYS == 0
    n_pairs = SB_W // LANES
    base = 4 * RET_W // LANES
    idx = np.arange(SB_KEYS)
    tri = jnp.asarray((idx[:, None] > idx[None, :]).astype(np.float32), BF16)
    return pl.pallas_call(
        _stickbreak_kernel,
        out_shape=jax.ShapeDtypeStruct((bsz, seq, SB_W), BF16),
        grid=(bsz, n_pairs, seq // SB_KEYS),
        in_specs=[
            pl.BlockSpec((1, SB_KEYS, LANES), lambda b, p, i: (b, i, base + p)),
            pl.BlockSpec((1, seq, LANES), lambda b, p, i: (b, 0, base + n_pairs + p)),
            pl.BlockSpec((1, seq, LANES), lambda b, p, i: (b, 0, base + 2 * n_pairs + p)),
            _resident((SB_KEYS, SB_KEYS)),
        ],
        out_specs=pl.BlockSpec((1, SB_KEYS, LANES), lambda b, p, i: (b, i, p)),
        scratch_shapes=[pltpu.VMEM((SB_PAIR, SB_KEYS, LANES), F32),
                        pltpu.VMEM((SB_PAIR, SB_KEYS, LANES), F32)],
        compiler_params=_params(("parallel", "parallel", "arbitrary")),
        name="stickbreak",
    )(h3, h3, h3, tri)


DA_TQ = 512
DA_TK = 512
DA_QB = DA_TQ // BLOCK
DA_KB = DA_TK // BLOCK


def _bucket_tiles():
    rel = np.arange(2 * BLOCK)
    max_exact = REL_BUCKETS // 2
    nf = np.maximum(rel, 1).astype(np.float32)
    large = max_exact + (np.log(nf / np.float32(max_exact)) / np.float32(math.log(REL_MAX_DIST / max_exact))
                         * np.float32(REL_BUCKETS - max_exact)).astype(np.int32)
    bucket = np.where(rel < max_exact, rel, np.minimum(large, REL_BUCKETS - 1)).astype(np.int32)
    assert bucket[BLOCK - 1:].min() == REL_BUCKETS - 1
    idx = np.arange(BLOCK)
    d = idx[:, None] - idx[None, :]
    return np.stack([bucket[np.maximum(d, 0)], bucket[d + BLOCK]]).astype(np.int32)


def _diffattn_kernel(tab_ref, q_ref, k_ref, v_ref, bidx_ref, lam_ref, subg_ref, o_ref,
                     bias_ref, vext_ref, acc_ref, m_ref, s_ref, *, lam_init):
    head, b, qi = pl.program_id(0), pl.program_id(1), pl.program_id(2)
    far = REL_BUCKETS - 1

    @pl.when((b == 0) & (qi == 0))
    def _():
        row = lax.broadcasted_iota(jnp.int32, (BLOCK, BLOCK), 0)
        col = lax.broadcasted_iota(jnp.int32, (BLOCK, BLOCK), 1)
        bias_ref[0] = jnp.full((BLOCK, BLOCK), NEG_BIG, F32)
        for t in range(2):
            idx = bidx_ref[t]
            bias = jnp.zeros((BLOCK, BLOCK), F32)
            for bk in range(REL_BUCKETS):
                bias = jnp.where(idx == bk, tab_ref[bk, head] - tab_ref[far, head], bias)
            if t == 0:
                bias = jnp.where(col <= row, bias, NEG_BIG)
            bias_ref[1 + t] = bias
        bias_ref[3] = jnp.zeros((BLOCK, BLOCK), F32)

    @pl.when(qi == 0)
    def _():
        vext_ref[:, :LANES] = v_ref[0]
        vext_ref[:, LANES:] = jnp.ones((vext_ref.shape[0], LANES), vext_ref.dtype)

    q = q_ref[0] * (DIFF_DIM ** -0.5)
    lane = lax.broadcasted_iota(jnp.int32, q.shape, 1)
    q_maps = [jnp.where((lane // DIFF_DIM) == p, q, jnp.zeros_like(q)) for p in range(2)]
    acc_ref[...] = jnp.zeros_like(acc_ref)
    m_ref[...] = jnp.full(m_ref.shape, NEG_BIG, F32)

    def scores(j):
        start = pl.multiple_of(j * DA_TK, DA_TK)
        kt = k_ref[0, pl.ds(start, DA_TK), :]
        for p in range(2):
            s_ref[p] = _dot_nt(q_maps[p], kt)

    def absorb(j, near):
        start = pl.multiple_of(j * DA_TK, DA_TK)
        vt = vext_ref[pl.ds(start, DA_TK), :]
        if near:
            rows = []
            for rb in range(DA_QB):
                dist = (qi * DA_QB + rb) - j * DA_KB
                rows.append(jnp.concatenate(
                    [bias_ref[jnp.clip(dist - cb, -1, 2) + 1] for cb in range(DA_KB)], axis=1))
            bias = jnp.concatenate(rows, axis=0)
        maps = range(2)
        s = [s_ref[p] + bias if near else s_ref[p] for p in maps]
        m_old = [m_ref[p] for p in maps]
        m_new = [jnp.maximum(m_old[p], jnp.max(s[p], axis=-1, keepdims=True)) for p in maps]
        e = [jnp.exp(s[p] - _lane_tile(m_new[p], DA_KB)).astype(BF16) for p in maps]
        for p in maps:
            m_ref[p] = m_new[p]
            acc_ref[p] = acc_ref[p] * _lane_tile(jnp.exp(m_old[p] - m_new[p]), 2) + _dot(e[p], vt)

    def step(j, carry, near):
        absorb(j, near)
        scores(j + 1)
        return carry

    n_tiles = (qi * DA_TQ + DA_TQ - 1) // DA_TK + 1
    n_far = jnp.maximum((qi * DA_QB - 1) // DA_KB, 0)
    scores(0)
    lax.fori_loop(0, n_far, functools.partial(step, near=False), 0)
    lax.fori_loop(n_far, n_tiles - 1, functools.partial(step, near=True), 0)
    absorb(n_tiles - 1, True)

    lam_v = lam_ref[...]
    lam = (jnp.exp(jnp.sum(lam_v[0:1] * lam_v[1:2], axis=-1, keepdims=True))
           - jnp.exp(jnp.sum(lam_v[2:3] * lam_v[3:4], axis=-1, keepdims=True)) + lam_init)
    o0 = acc_ref[0, :, :LANES] / acc_ref[0, :, LANES:]
    o1 = acc_ref[1, :, :LANES] / acc_ref[1, :, LANES:]
    o = o0 - lam * o1
    o = o * lax.rsqrt(jnp.mean(o * o, axis=-1, keepdims=True) + LN_EPS) * subg_ref[...]
    o_ref[0] = (o * (1.0 - lam_init)).astype(o_ref.dtype)


def _diffattn(h3, rel_bias, lam_params, sub_g, lam_init):
    bsz, seq, _ = h3.shape
    assert seq % DA_TK == 0 and DA_TK % DA_TQ == 0
    return pl.pallas_call(
        functools.partial(_diffattn_kernel, lam_init=lam_init),
        out_shape=jax.ShapeDtypeStruct((bsz, seq, DIFF_W), BF16),
        grid=(DIFF_HEADS, bsz, seq // DA_TQ),
        in_specs=[
            pl.BlockSpec(memory_space=pltpu.SMEM),
            pl.BlockSpec((1, DA_TQ, LANES), lambda h, b, i: (b, i, h)),
            pl.BlockSpec((1, seq, LANES), lambda h, b, i: (b, 0, DIFF_HEADS + h)),
            pl.BlockSpec((1, seq, LANES), lambda h, b, i: (b, 0, 2 * DIFF_HEADS + h)),
            _resident((2, BLOCK, BLOCK)),
            _resident((4, DIFF_DIM)),
            _resident((1, 2 * DIFF_DIM)),
        ],
        out_specs=pl.BlockSpec((1, DA_TQ, LANES), lambda h, b, i: (b, i, h)),
        scratch_shapes=[pltpu.VMEM((4, BLOCK, BLOCK), F32),
                        pltpu.VMEM((seq, 2 * LANES), BF16),
                        pltpu.VMEM((2, DA_TQ, 2 * LANES), F32),
                        pltpu.VMEM((2, DA_TQ, LANES), F32),
                        pltpu.VMEM((2, DA_TQ, DA_TK), F32)],
        compiler_params=_params(("arbitrary", "arbitrary", "arbitrary")),
        name="diffattn",
    )(rel_bias, h3, h3, h3, jnp.asarray(_bucket_tiles()), lam_params, sub_g)


def _proj_ln_kernel(*refs, n_in):
    a_refs = refs[:n_in]
    w_ref, x_ref, g_ref, b_ref, o_ref, obf_ref, wb_ref = refs[n_in:]
    _cast_weight_once(w_ref, wb_ref, w_ref.shape[1])
    y = ALPHA * x_ref[...]
    row = 0
    for a_ref in a_refs:
        width = a_ref.shape[1]
        y = y + _dot(a_ref[...], wb_ref[row:row + width, :])
        row += width
    mu = jnp.mean(y, axis=-1, keepdims=True)
    cen = y - mu
    var = jnp.mean(cen * cen, axis=-1, keepdims=True)
    out = cen * lax.rsqrt(var + LN_EPS) * g_ref[...] + b_ref[...]
    o_ref[...] = out
    obf_ref[...] = out.astype(obf_ref.dtype)


def _proj_ln(acts, weight, x, gain, bias, tm=ROW_TILE):
    m, d = x.shape
    assert m % tm == 0 and sum(a.shape[1] for a in acts) == weight.shape[0]
    n_in = len(acts)
    in_specs = [pl.BlockSpec((tm, a.shape[1]), lambda i: (i, 0)) for a in acts]
    in_specs += [_resident(weight.shape)]
    in_specs += [pl.BlockSpec((tm, d), lambda i: (i, 0)), _resident((1, d)), _resident((1, d))]
    return pl.pallas_call(
        functools.partial(_proj_ln_kernel, n_in=n_in),
        out_shape=(jax.ShapeDtypeStruct((m, d), F32), jax.ShapeDtypeStruct((m, d), BF16)),
        grid=(m // tm,),
        in_specs=in_specs,
        out_specs=(pl.BlockSpec((tm, d), lambda i: (i, 0)), pl.BlockSpec((tm, d), lambda i: (i, 0))),
        scratch_shapes=[pltpu.VMEM(weight.shape, BF16)],
        compiler_params=_params(("arbitrary",)),
        name="proj_ln",
    )(*acts, weight, x, gain.reshape(1, d), bias.reshape(1, d))


def _shift_rows(h, prev, n):
    row = lax.broadcasted_iota(jnp.int32, h.shape, 0)
    out = pltpu.roll(h, n, axis=0)
    for r in range(n):
        out = jnp.where(row == r, prev[SUBLANES - n + r:SUBLANES - n + r + 1], out)
    return out


def _ffn_up_kernel(x_ref, w_ref, cw_ref, cb_ref, o_ref, carry_ref, wb_ref, *, tiles_per_seq):
    _cast_weight_once(w_ref, wb_ref, 2 * MXU_N)
    x = x_ref[...]
    tm = x.shape[0]
    seq_start = (pl.program_id(0) % tiles_per_seq) == 0
    for c in range(D_FF // MXU_N):
        halves = []
        for base in (c * MXU_N, D_FF + c * MXU_N):
            cols = slice(base, base + MXU_N)
            h = _dot(x, wb_ref[:, cols])
            prev = jnp.where(seq_start, 0.0, carry_ref[:, cols])
            carry_ref[:, cols] = h[tm - SUBLANES:]
            cw = cw_ref[:, cols]
            halves.append(_shift_rows(h, prev, 2) * cw[0:1] + _shift_rows(h, prev, 1) * cw[1:2]
                          + h * cw[2:3] + cb_ref[:, cols])
        u, g = halves
        o_ref[:, c * MXU_N:(c + 1) * MXU_N] = (g * jax.nn.sigmoid(g) * u).astype(o_ref.dtype)


def _ffn_up(x_bf, w_up, conv_w, conv_b, seq, tm=ROW_TILE):
    m, d = x_bf.shape
    assert seq % tm == 0
    return pl.pallas_call(
        functools.partial(_ffn_up_kernel, tiles_per_seq=seq // tm),
        out_shape=jax.ShapeDtypeStruct((m, D_FF), BF16),
        grid=(m // tm,),
        in_specs=[pl.BlockSpec((tm, d), lambda i: (i, 0)), _resident((d, 2 * D_FF)),
                  _resident((CONV_W, 2 * D_FF)), _resident((1, 2 * D_FF))],
        out_specs=pl.BlockSpec((tm, D_FF), lambda i: (i, 0)),
        scratch_shapes=[pltpu.VMEM((SUBLANES, 2 * D_FF), F32), pltpu.VMEM((d, 2 * D_FF), BF16)],
        compiler_params=_params(("arbitrary",)),
        name="ffn_up",
    )(x_bf, w_up, conv_w, conv_b.reshape(1, 2 * D_FF))


def _rotary_tables(seq):
    inv = ROPE_BASE ** (-jnp.arange(0, RET_DIM, 2, dtype=F32) / RET_DIM)
    ang = jnp.arange(seq, dtype=F32)[:, None] * inv[None, :]
    cos, sin = jnp.cos(ang), jnp.sin(ang)
    return jnp.concatenate([cos, cos], -1), jnp.concatenate([-sin, sin], -1)


def kernel(x, w_in_even, w_out_even, w_in_odd, w_out_odd, lam_q1, lam_k1, lam_q2, lam_k2, subln_g,
           rel_bias, w_up, conv_w, conv_b, w_down, ln1_g, ln1_b, ln2_g, ln2_b):
    bsz, seq, d = x.shape
    m = bsz * seq
    cos_t, sin_t = _rotary_tables(seq)
    xf = x.reshape(m, d)
    xb = xf
    for l in range(DEPTH):
        i = l // 2
        if l % 2 == 0:
            h = _in_proj(xb, w_in_even[i], cos_t, sin_t, seq, rotary_cols=2 * RET_W)
            h3 = h.reshape(bsz, seq, EVEN_IN)
            ret = _retention(h3).reshape(m, RET_W)
            sb = _stickbreak(h3).reshape(m, SB_W)
            xf, xb = _proj_ln([ret, sb], w_out_even[i], xf, ln1_g[l], ln1_b[l])
        else:
            h = _in_proj(xb, w_in_odd[i], cos_t, sin_t, seq, rotary_cols=0)
            lam_params = jnp.stack([lam_q1[i], lam_k1[i], lam_q2[i], lam_k2[i]]).astype(F32)
            lam_init = 0.8 - 0.6 * math.exp(-0.3 * l)
            o = _diffattn(h.reshape(bsz, seq, ODD_IN), rel_bias.astype(F32), lam_params,
                          subln_g[i].reshape(1, 2 * DIFF_DIM).astype(F32), lam_init)
            xf, xb = _proj_ln([o.reshape(m, DIFF_W)], w_out_odd[i], xf, ln1_g[l], ln1_b[l])
        f = _ffn_up(xb, w_up[l], conv_w[l], conv_b[l], seq)
        xf, xb = _proj_ln([f], w_down[l], xf, ln2_g[l], ln2_b[l])
    return xf.reshape(bsz, seq, d)
```

```python
import functools
import math

import numpy as np
import jax
import jax.numpy as jnp
from jax import lax
from jax.experimental import pallas as pl
from jax.experimental.pallas import tpu as pltpu

D_MODEL = 1024
DEPTH = 4
BLOCK = 128
RET_HEADS = 4
RET_DIM = 128
SB_HEADS = 8
SB_DIM = 64
DIFF_HEADS = 8
DIFF_DIM = 64
REL_BUCKETS = 32
REL_MAX_DIST = 128
D_FF = 2816
CONV_W = 3
ALPHA = (2 * DEPTH) ** 0.25
LN_EPS = 1e-5
ROPE_BASE = 10000.0

RET_W = RET_HEADS * RET_DIM
SB_W = SB_HEADS * SB_DIM
EVEN_IN = 4 * RET_W + 3 * SB_W
DIFF_W = DIFF_HEADS * 2 * DIFF_DIM
ODD_IN = 3 * DIFF_W

LANES = 128
SUBLANES = 8
MXU_N = 256
VMEM_LIMIT = 56 * 1024 * 1024
NEG_BIG = -1e30
ROW_TILE = 512

F32 = jnp.float32
BF16 = jnp.bfloat16


def _params(semantics):
    return pltpu.CompilerParams(dimension_semantics=semantics, vmem_limit_bytes=VMEM_LIMIT)


def _resident(shape):
    return pl.BlockSpec(shape, lambda *_: (0,) * len(shape), pipeline_mode=pl.Buffered(1))


def _resident_layer(w_stack, layer):
    return pl.BlockSpec((None,) + w_stack.shape[1:], lambda *_: (layer, 0, 0), pipeline_mode=pl.Buffered(1))


def _dot(a, b):
    return jnp.dot(a, b, preferred_element_type=F32)


def _dot_nt(a, b):
    return lax.dot_general(a, b, (((1,), (1,)), ((), ())), preferred_element_type=F32)


def _dot_tn(a, b):
    return lax.dot_general(a, b, (((0,), (0,)), ((), ())), preferred_element_type=F32)


def _lane_tile(x, reps):
    return x if reps == 1 else jnp.concatenate([x] * reps, axis=1)


def _stack_heads(q, width):
    lane = lax.broadcasted_iota(jnp.int32, q.shape, 1)
    zero = jnp.zeros_like(q)
    return jnp.concatenate([jnp.where((lane // width) == p, q, zero) for p in range(LANES // width)], axis=0)


IN_CHUNK = 512


def _cast_weight_once(w_ref, wb_ref, chunk):
    @pl.when(pl.program_id(0) == 0)
    def _():
        for lo in range(0, w_ref.shape[1], chunk):
            wb_ref[:, lo:lo + chunk] = w_ref[:, lo:lo + chunk].astype(wb_ref.dtype)


def _in_proj_kernel(x_ref, w_ref, cos_ref, sin_ref, o_ref, wb_ref, *, n_out, rotary_cols):
    _cast_weight_once(w_ref, wb_ref, IN_CHUNK)
    x = x_ref[...].astype(BF16)
    for c in range(n_out // IN_CHUNK):
        lo = c * IN_CHUNK
        acc = _dot(x, wb_ref[:, lo:lo + IN_CHUNK])
        if lo < rotary_cols:
            cosv, sinv = cos_ref[...], sin_ref[...]
            for hh in range(IN_CHUNK // RET_DIM):
                blk = acc[:, hh * RET_DIM:(hh + 1) * RET_DIM]
                rot = blk * cosv + pltpu.roll(blk, RET_DIM // 2, axis=1) * sinv
                o_ref[:, lo + hh * RET_DIM:lo + (hh + 1) * RET_DIM] = rot.astype(o_ref.dtype)
        else:
            o_ref[:, lo:lo + IN_CHUNK] = acc.astype(o_ref.dtype)


def _in_proj(x_bf, w, layer, cos_t, sin_t, seq, rotary_cols, tm=ROW_TILE):
    m, k = x_bf.shape
    n_out = w.shape[2]
    assert seq % tm == 0 and n_out % IN_CHUNK == 0
    s_tiles = seq // tm
    return pl.pallas_call(
        functools.partial(_in_proj_kernel, n_out=n_out, rotary_cols=rotary_cols),
        out_shape=jax.ShapeDtypeStruct((m, n_out), BF16),
        grid=(m // tm,),
        in_specs=[
            pl.BlockSpec((tm, k), lambda i: (i, 0)),
            _resident_layer(w, layer),
            pl.BlockSpec((tm, RET_DIM), lambda i: (i % s_tiles, 0)),
            pl.BlockSpec((tm, RET_DIM), lambda i: (i % s_tiles, 0)),
        ],
        out_specs=pl.BlockSpec((tm, n_out), lambda i: (i, 0)),
        scratch_shapes=[pltpu.VMEM((k, n_out), BF16)],
        compiler_params=_params(("arbitrary",)),
        name="in_proj",
    )(x_bf, w, cos_t, sin_t)


RET_CHUNK = 256


def _retention_tables():
    hh = np.arange(RET_HEADS, dtype=np.float32)
    log_g = np.log(np.float32(1.0) - np.float32(2.0) ** (np.float32(-5.0) - hh)).astype(np.float32)
    idx = np.arange(RET_CHUNK, dtype=np.float32)
    rel = idx[:, None] - idx[None, :]
    scale = np.float32(RET_DIM ** -0.5)
    decay = np.where(rel >= 0, np.exp(log_g[:, None, None] * np.maximum(rel, 0.0)), 0.0) * scale
    q_dec = np.exp(log_g[:, None] * (idx[None, :] + 1.0))
    k_dec = np.exp(log_g[:, None] * (RET_CHUNK - 1.0 - idx[None, :])) * scale
    chunk_g = np.exp(log_g * RET_CHUNK)
    bcast = lambda t: np.broadcast_to(t[:, :, None], (RET_HEADS, RET_CHUNK, LANES))
    return (decay.astype(np.float32), np.ascontiguousarray(bcast(q_dec), np.float32),
            np.ascontiguousarray(bcast(k_dec), np.float32), [float(g) for g in chunk_g])


def _retention_kernel(q_ref, k_ref, v_ref, g_ref, decay_ref, qdec_ref, kdec_ref, o_ref, state_ref,
                      *, chunk_g):
    @pl.when(pl.program_id(1) == 0)
    def _():
        state_ref[...] = jnp.zeros_like(state_ref)

    for hh in range(RET_HEADS):
        cols = slice(hh * RET_DIM, (hh + 1) * RET_DIM)
        q, k, v = q_ref[0, :, cols], k_ref[0, :, cols], v_ref[0, :, cols]
        state = state_ref[hh]
        s = _dot_nt(q, k) * decay_ref[hh]
        out = _dot(s.astype(BF16), v) + _dot(q, state.astype(BF16)) * qdec_ref[hh]
        k_scaled = (k.astype(F32) * kdec_ref[hh]).astype(BF16)
        state_ref[hh] = state * chunk_g[hh] + _dot_tn(k_scaled, v)
        mu = jnp.mean(out, axis=-1, keepdims=True)
        cen = out - mu
        var = jnp.mean(cen * cen, axis=-1, keepdims=True)
        gate = g_ref[0, :, cols].astype(F32)
        o_ref[0, :, cols] = (cen * lax.rsqrt(var + LN_EPS) * (gate * jax.nn.sigmoid(gate))).astype(o_ref.dtype)


def _retention(h3):
    bsz, seq, _ = h3.shape
    decay, q_dec, k_dec, chunk_g = _retention_tables()
    assert seq % RET_CHUNK == 0
    col = lambda j: pl.BlockSpec((1, RET_CHUNK, RET_W), lambda b, c, j=j: (b, c, j))
    return pl.pallas_call(
        functools.partial(_retention_kernel, chunk_g=chunk_g),
        out_shape=jax.ShapeDtypeStruct((bsz, seq, RET_W), BF16),
        grid=(bsz, seq // RET_CHUNK),
        in_specs=[col(0), col(1), col(2), col(3),
                  _resident((RET_HEADS, RET_CHUNK, RET_CHUNK)), _resident((RET_HEADS, RET_CHUNK, LANES)),
                  _resident((RET_HEADS, RET_CHUNK, LANES))],
        out_specs=pl.BlockSpec((1, RET_CHUNK, RET_W), lambda b, c: (b, c, 0)),
        scratch_shapes=[pltpu.VMEM((RET_HEADS, RET_DIM, RET_DIM), F32)],
        compiler_params=_params(("parallel", "arbitrary")),
        name="retention",
    )(h3, h3, h3, h3, jnp.asarray(decay), jnp.asarray(q_dec), jnp.asarray(k_dec))


SB_PAIR = LANES // SB_DIM
SB_KEYS = MXU_N
SB_DEAD = -104.0


def _split_bf16(x):
    hi = x.astype(BF16)
    lo = (x - hi.astype(F32)).astype(BF16)
    return hi, lo


def _stickbreak_kernel(q_ref, k_ref, v_ref, tri_ref, o_ref, acc_ref, run_ref):
    qi = pl.program_id(2)
    q = q_ref[0] * (SB_DIM ** -0.5)
    lane = lax.broadcasted_iota(jnp.int32, q.shape, 1)
    q_heads = [jnp.where((lane // SB_DIM) == p, q, jnp.zeros_like(q)) for p in range(SB_PAIR)]
    tri = tri_ref[...]

    r = lax.broadcasted_iota(jnp.int32, (SB_KEYS, SB_KEYS), 0)
    c = lax.broadcasted_iota(jnp.int32, (SB_KEYS, SB_KEYS), 1)
    past = c < r

    def add_blocks(blocks, runs, accs):
        kts, vts = [], []
        for jb, _, _ in blocks:
            start = pl.multiple_of(jb * SB_KEYS, SB_KEYS)
            kts.append(k_ref[0, pl.ds(start, SB_KEYS), :])
            vts.append(v_ref[0, pl.ds(start, SB_KEYS), :])
        chains = [(b, p) for b in range(len(blocks)) for p in range(SB_PAIR)]

        def masked(b, x):
            _, diagonal, valid = blocks[b]
            if diagonal:
                x = jnp.where(past, x, 0.0)
            if valid is not None:
                x = jnp.where(valid, x, 0.0)
            return x

        z = {bp: _dot_nt(q_heads[bp[1]], kts[bp[0]]) for bp in chains}
        log_take = {bp: jnp.minimum(z[bp], 0.0) - jnp.log(1.0 + jnp.exp(-jnp.abs(z[bp])))
                    for bp in chains}
        log_keep = {bp: masked(bp[0], log_take[bp] - z[bp]) for bp in chains}
        parts = {bp: _split_bf16(log_keep[bp]) for bp in chains}
        inner = {bp: _dot(parts[bp][0], tri) + _dot(parts[bp][1], tri) for bp in chains}
        runs, accs = list(runs), list(accs)
        weights = {}
        for b, p in chains:
            between = inner[b, p] + _lane_tile(runs[p], SB_KEYS // LANES)
            weights[b, p] = masked(b, jnp.exp(log_take[b, p] + between)).astype(BF16)
            runs[p] = runs[p] + jnp.sum(log_keep[b, p], axis=-1, keepdims=True)
        for b, p in chains:
            accs[p] = accs[p] + _dot(weights[b, p], vts[b])
        return runs, accs

    def any_alive(runs):
        return jnp.max(functools.reduce(jnp.maximum, runs)) > SB_DEAD

    zeros = [jnp.zeros((SB_KEYS, LANES), F32)] * SB_PAIR
    runs, accs = add_blocks([(qi, True, None), (jnp.maximum(qi - 1, 0), False, qi > 0)], zeros, zeros)
    for p in range(SB_PAIR):
        run_ref[p], acc_ref[p] = runs[p], accs[p]

    def more(carry):
        jb, _ = carry
        runs, accs = add_blocks([(jb, False, None)], [run_ref[p] for p in range(SB_PAIR)],
                                [acc_ref[p] for p in range(SB_PAIR)])
        for p in range(SB_PAIR):
            run_ref[p], acc_ref[p] = runs[p], accs[p]
        return jb - 1, any_alive(runs)

    lax.while_loop(lambda c: (c[0] >= 0) & c[1], more, (qi - 2, any_alive(runs)))
    out = acc_ref[0]
    for p in range(1, SB_PAIR):
        out = jnp.where((lane // SB_DIM) == p, acc_ref[p], out)
    o_ref[0] = out.astype(o_ref.dtype)


def _stickbreak(h3):
    bsz, seq, _ = h3.shape
    assert seq % SB_KEYS == 0
    n_pairs = SB_W // LANES
    base = 4 * RET_W // LANES
    idx = np.arange(SB_KEYS)
    tri = jnp.asarray((idx[:, None] > idx[None, :]).astype(np.float32), BF16)
    return pl.pallas_call(
        _stickbreak_kernel,
        out_shape=jax.ShapeDtypeStruct((bsz, seq, SB_W), BF16),
        grid=(bsz, n_pairs, seq // SB_KEYS),
        in_specs=[
            pl.BlockSpec((1, SB_KEYS, LANES), lambda b, p, i: (b, i, base + p)),
            pl.BlockSpec((1, seq, LANES), lambda b, p, i: (b, 0, base + n_pairs + p)),
            pl.BlockSpec((1, seq, LANES), lambda b, p, i: (b, 0, base + 2 * n_pairs + p)),
            _resident((SB_KEYS, SB_KEYS)),
        ],
        out_specs=pl.BlockSpec((1, SB_KEYS, LANES), lambda b, p, i: (b, i, p)),
        scratch_shapes=[pltpu.VMEM((SB_PAIR, SB_KEYS, LANES), F32),
                        pltpu.VMEM((SB_PAIR, SB_KEYS, LANES), F32)],
        compiler_params=_params(("parallel", "parallel", "arbitrary")),
        name="stickbreak",
    )(h3, h3, h3, tri)


DA_TQ = 512
DA_TK = 512
DA_QB = DA_TQ // BLOCK
DA_KB = DA_TK // BLOCK


def _bucket_tiles():
    rel = np.arange(2 * BLOCK)
    max_exact = REL_BUCKETS // 2
    nf = np.maximum(rel, 1).astype(np.float32)
    large = max_exact + (np.log(nf / np.float32(max_exact)) / np.float32(math.log(REL_MAX_DIST / max_exact))
                         * np.float32(REL_BUCKETS - max_exact)).astype(np.int32)
    bucket = np.where(rel < max_exact, rel, np.minimum(large, REL_BUCKETS - 1)).astype(np.int32)
    assert bucket[BLOCK - 1:].min() == REL_BUCKETS - 1
    idx = np.arange(BLOCK)
    d = idx[:, None] - idx[None, :]
    return np.stack([bucket[np.maximum(d, 0)], bucket[d + BLOCK]]).astype(np.int32)


def _diffattn_kernel(tab_ref, q_ref, k_ref, v_ref, bidx_ref, lam_ref, subg_ref, o_ref,
                     bias_ref, vext_ref, acc_ref, m_ref, s_ref, *, lam_init):
    head, b, qi = pl.program_id(0), pl.program_id(1), pl.program_id(2)
    far = REL_BUCKETS - 1

    @pl.when((b == 0) & (qi == 0))
    def _():
        row = lax.broadcasted_iota(jnp.int32, (BLOCK, BLOCK), 0)
        col = lax.broadcasted_iota(jnp.int32, (BLOCK, BLOCK), 1)
        bias_ref[0] = jnp.full((BLOCK, BLOCK), NEG_BIG, F32)
        for t in range(2):
            idx = bidx_ref[t]
            bias = jnp.zeros((BLOCK, BLOCK), F32)
            for bk in range(REL_BUCKETS):
                bias = jnp.where(idx == bk, tab_ref[bk, head] - tab_ref[far, head], bias)
            if t == 0:
                bias = jnp.where(col <= row, bias, NEG_BIG)
            bias_ref[1 + t] = bias
        bias_ref[3] = jnp.zeros((BLOCK, BLOCK), F32)

    @pl.when(qi == 0)
    def _():
        vext_ref[:, :LANES] = v_ref[0]
        vext_ref[:, LANES:] = jnp.ones((vext_ref.shape[0], LANES), vext_ref.dtype)

    q = q_ref[0] * (DIFF_DIM ** -0.5)
    lane = lax.broadcasted_iota(jnp.int32, q.shape, 1)
    q_maps = [jnp.where((lane // DIFF_DIM) == p, q, jnp.zeros_like(q)) for p in range(2)]
    acc_ref[...] = jnp.zeros_like(acc_ref)
    m_ref[...] = jnp.full(m_ref.shape, NEG_BIG, F32)

    def scores(j):
        start = pl.multiple_of(j * DA_TK, DA_TK)
        kt = k_ref[0, pl.ds(start, DA_TK), :]
        for p in range(2):
            s_ref[p] = _dot_nt(q_maps[p], kt)

    def absorb(j, near):
        start = pl.multiple_of(j * DA_TK, DA_TK)
        vt = vext_ref[pl.ds(start, DA_TK), :]
        if near:
            rows = []
            for rb in range(DA_QB):
                dist = (qi * DA_QB + rb) - j * DA_KB
                rows.append(jnp.concatenate(
                    [bias_ref[jnp.clip(dist - cb, -1, 2) + 1] for cb in range(DA_KB)], axis=1))
            bias = jnp.concatenate(rows, axis=0)
        maps = range(2)
        s = [s_ref[p] + bias if near else s_ref[p] for p in maps]
        m_old = [m_ref[p] for p in maps]
        m_new = [jnp.maximum(m_old[p], jnp.max(s[p], axis=-1, keepdims=True)) for p in maps]
        e = [jnp.exp(s[p] - _lane_tile(m_new[p], DA_KB)).astype(BF16) for p in maps]
        for p in maps:
            m_ref[p] = m_new[p]
            acc_ref[p] = acc_ref[p] * _lane_tile(jnp.exp(m_old[p] - m_new[p]), 2) + _dot(e[p], vt)

    def step(j, carry, near):
        absorb(j, near)
        scores(j + 1)
        return carry

    n_tiles = (qi * DA_TQ + DA_TQ - 1) // DA_TK + 1
    n_far = jnp.maximum((qi * DA_QB - 1) // DA_KB, 0)
    scores(0)
    lax.fori_loop(0, n_far, functools.partial(step, near=False), 0)
    lax.fori_loop(n_far, n_tiles - 1, functools.partial(step, near=True), 0)
    absorb(n_tiles - 1, True)

    lam_v = lam_ref[...]
    lam = (jnp.exp(jnp.sum(lam_v[0:1] * lam_v[1:2], axis=-1, keepdims=True))
           - jnp.exp(jnp.sum(lam_v[2:3] * lam_v[3:4], axis=-1, keepdims=True)) + lam_init)
    o0 = acc_ref[0, :, :LANES] / acc_ref[0, :, LANES:]
    o1 = acc_ref[1, :, :LANES] / acc_ref[1, :, LANES:]
    o = o0 - lam * o1
    o = o * lax.rsqrt(jnp.mean(o * o, axis=-1, keepdims=True) + LN_EPS) * subg_ref[...]
    o_ref[0] = (o * (1.0 - lam_init)).astype(o_ref.dtype)


def _diffattn(h3, rel_bias, lam_params, sub_g, lam_init):
    bsz, seq, _ = h3.shape
    assert seq % DA_TK == 0 and DA_TK % DA_TQ == 0
    return pl.pallas_call(
        functools.partial(_diffattn_kernel, lam_init=lam_init),
        out_shape=jax.ShapeDtypeStruct((bsz, seq, DIFF_W), BF16),
        grid=(DIFF_HEADS, bsz, seq // DA_TQ),
        in_specs=[
            pl.BlockSpec(memory_space=pltpu.SMEM),
            pl.BlockSpec((1, DA_TQ, LANES), lambda h, b, i: (b, i, h)),
            pl.BlockSpec((1, seq, LANES), lambda h, b, i: (b, 0, DIFF_HEADS + h)),
            pl.BlockSpec((1, seq, LANES), lambda h, b, i: (b, 0, 2 * DIFF_HEADS + h)),
            _resident((2, BLOCK, BLOCK)),
            _resident((4, DIFF_DIM)),
            _resident((1, 2 * DIFF_DIM)),
        ],
        out_specs=pl.BlockSpec((1, DA_TQ, LANES), lambda h, b, i: (b, i, h)),
        scratch_shapes=[pltpu.VMEM((4, BLOCK, BLOCK), F32),
                        pltpu.VMEM((seq, 2 * LANES), BF16),
                        pltpu.VMEM((2, DA_TQ, 2 * LANES), F32),
                        pltpu.VMEM((2, DA_TQ, LANES), F32),
                        pltpu.VMEM((2, DA_TQ, DA_TK), F32)],
        compiler_params=_params(("arbitrary", "arbitrary", "arbitrary")),
        name="diffattn",
    )(rel_bias, h3, h3, h3, jnp.asarray(_bucket_tiles()), lam_params, sub_g)


def _proj_ln_kernel(*refs, n_in):
    a_refs = refs[:n_in]
    w_ref, x_ref, g_ref, b_ref, o_ref, obf_ref, wb_ref = refs[n_in:]
    _cast_weight_once(w_ref, wb_ref, w_ref.shape[1])
    y = ALPHA * x_ref[...]
    row = 0
    for a_ref in a_refs:
        width = a_ref.shape[1]
        y = y + _dot(a_ref[...], wb_ref[row:row + width, :])
        row += width
    mu = jnp.mean(y, axis=-1, keepdims=True)
    cen = y - mu
    var = jnp.mean(cen * cen, axis=-1, keepdims=True)
    out = cen * lax.rsqrt(var + LN_EPS) * g_ref[...] + b_ref[...]
    o_ref[...] = out
    obf_ref[...] = out.astype(obf_ref.dtype)


def _proj_ln(acts, weight, layer, x, gain, bias, tm=ROW_TILE):
    m, d = x.shape
    assert m % tm == 0 and sum(a.shape[1] for a in acts) == weight.shape[1]
    n_in = len(acts)
    in_specs = [pl.BlockSpec((tm, a.shape[1]), lambda i: (i, 0)) for a in acts]
    in_specs += [_resident_layer(weight, layer)]
    in_specs += [pl.BlockSpec((tm, d), lambda i: (i, 0)), _resident((1, d)), _resident((1, d))]
    return pl.pallas_call(
        functools.partial(_proj_ln_kernel, n_in=n_in),
        out_shape=(jax.ShapeDtypeStruct((m, d), F32), jax.ShapeDtypeStruct((m, d), BF16)),
        grid=(m // tm,),
        in_specs=in_specs,
        out_specs=(pl.BlockSpec((tm, d), lambda i: (i, 0)), pl.BlockSpec((tm, d), lambda i: (i, 0))),
        scratch_shapes=[pltpu.VMEM(weight.shape[1:], BF16)],
        compiler_params=_params(("arbitrary",)),
        name="proj_ln",
    )(*acts, weight, x, gain.reshape(1, d), bias.reshape(1, d))


def _shift_rows(h, prev, n):
    row = lax.broadcasted_iota(jnp.int32, h.shape, 0)
    out = pltpu.roll(h, n, axis=0)
    for r in range(n):
        out = jnp.where(row == r, prev[SUBLANES - n + r:SUBLANES - n + r + 1], out)
    return out


def _ffn_up_kernel(x_ref, w_ref, cw_ref, cb_ref, o_ref, carry_ref, wb_ref, *, tiles_per_seq):
    _cast_weight_once(w_ref, wb_ref, 2 * MXU_N)
    x = x_ref[...]
    tm = x.shape[0]
    seq_start = (pl.program_id(0) % tiles_per_seq) == 0
    for c in range(D_FF // MXU_N):
        halves = []
        for base in (c * MXU_N, D_FF + c * MXU_N):
            cols = slice(base, base + MXU_N)
            h = _dot(x, wb_ref[:, cols])
            prev = jnp.where(seq_start, 0.0, carry_ref[:, cols])
            carry_ref[:, cols] = h[tm - SUBLANES:]
            cw = cw_ref[:, cols]
            halves.append(_shift_rows(h, prev, 2) * cw[0:1] + _shift_rows(h, prev, 1) * cw[1:2]
                          + h * cw[2:3] + cb_ref[:, cols])
        u, g = halves
        o_ref[:, c * MXU_N:(c + 1) * MXU_N] = (g * jax.nn.sigmoid(g) * u).astype(o_ref.dtype)


def _ffn_up(x_bf, w_up, layer, conv_w, conv_b, seq, tm=ROW_TILE):
    m, d = x_bf.shape
    assert seq % tm == 0
    return pl.pallas_call(
        functools.partial(_ffn_up_kernel, tiles_per_seq=seq // tm),
        out_shape=jax.ShapeDtypeStruct((m, D_FF), BF16),
        grid=(m // tm,),
        in_specs=[pl.BlockSpec((tm, d), lambda i: (i, 0)), _resident_layer(w_up, layer),
                  _resident((CONV_W, 2 * D_FF)), _resident((1, 2 * D_FF))],
        out_specs=pl.BlockSpec((tm, D_FF), lambda i: (i, 0)),
        scratch_shapes=[pltpu.VMEM((SUBLANES, 2 * D_FF), F32), pltpu.VMEM((d, 2 * D_FF), BF16)],
        compiler_params=_params(("arbitrary",)),
        name="ffn_up",
    )(x_bf, w_up, conv_w, conv_b.reshape(1, 2 * D_FF))


def _rotary_tables(seq):
    inv = ROPE_BASE ** (-jnp.arange(0, RET_DIM, 2, dtype=F32) / RET_DIM)
    ang = jnp.arange(seq, dtype=F32)[:, None] * inv[None, :]
    cos, sin = jnp.cos(ang), jnp.sin(ang)
    return jnp.concatenate([cos, cos], -1), jnp.concatenate([-sin, sin], -1)


def kernel(x, w_in_even, w_out_even, w_in_odd, w_out_odd, lam_q1, lam_k1, lam_q2, lam_k2, subln_g,
           rel_bias, w_up, conv_w, conv_b, w_down, ln1_g, ln1_b, ln2_g, ln2_b):
    bsz, seq, d = x.shape
    m = bsz * seq
    cos_t, sin_t = _rotary_tables(seq)
    xf = x.reshape(m, d)
    xb = xf
    for l in range(DEPTH):
        i = l // 2
        if l % 2 == 0:
            h = _in_proj(xb, w_in_even, i, cos_t, sin_t, seq, rotary_cols=2 * RET_W)
            h3 = h.reshape(bsz, seq, EVEN_IN)
            ret = _retention(h3).reshape(m, RET_W)
            sb = _stickbreak(h3).reshape(m, SB_W)
            xf, xb = _proj_ln([ret, sb], w_out_even, i, xf, ln1_g[l], ln1_b[l])
        else:
            h = _in_proj(xb, w_in_odd, i, cos_t, sin_t, seq, rotary_cols=0)
            lam_params = jnp.stack([lam_q1[i], lam_k1[i], lam_q2[i], lam_k2[i]]).astype(F32)
            lam_init = 0.8 - 0.6 * math.exp(-0.3 * l)
            o = _diffattn(h.reshape(bsz, seq, ODD_IN), rel_bias.astype(F32), lam_params,
                          subln_g[i].reshape(1, 2 * DIFF_DIM).astype(F32), lam_init)
            xf, xb = _proj_ln([o.reshape(m, DIFF_W)], w_out_odd, i, xf, ln1_g[l], ln1_b[l])
        f = _ffn_up(xb, w_up, l, conv_w[l], conv_b[l], seq)
        xf, xb = _proj_ln([f], w_down, l, xf, ln2_g[l], ln2_b[l])
    return xf.reshape(bsz, seq, d)
```

```python
import functools
import math

import numpy as np
import jax
import jax.numpy as jnp
from jax import lax
from jax.experimental import pallas as pl
from jax.experimental.pallas import tpu as pltpu

D_MODEL = 1024
DEPTH = 4
BLOCK = 128
RET_HEADS = 4
RET_DIM = 128
SB_HEADS = 8
SB_DIM = 64
DIFF_HEADS = 8
DIFF_DIM = 64
REL_BUCKETS = 32
REL_MAX_DIST = 128
D_FF = 2816
CONV_W = 3
ALPHA = (2 * DEPTH) ** 0.25
LN_EPS = 1e-5
ROPE_BASE = 10000.0

RET_W = RET_HEADS * RET_DIM
SB_W = SB_HEADS * SB_DIM
EVEN_IN = 4 * RET_W + 3 * SB_W
DIFF_W = DIFF_HEADS * 2 * DIFF_DIM
ODD_IN = 3 * DIFF_W

LANES = 128
SUBLANES = 8
MXU_N = 256
VMEM_LIMIT = 56 * 1024 * 1024
NEG_BIG = -1e30
ROW_TILE = 512
WIDE_ROW_TILE = 1024

F32 = jnp.float32
BF16 = jnp.bfloat16


def _params(semantics):
    return pltpu.CompilerParams(dimension_semantics=semantics, vmem_limit_bytes=VMEM_LIMIT)


def _resident(shape):
    return pl.BlockSpec(shape, lambda *_: (0,) * len(shape), pipeline_mode=pl.Buffered(1))


def _resident_layer(w_stack, layer):
    return pl.BlockSpec((None,) + w_stack.shape[1:], lambda *_: (layer, 0, 0), pipeline_mode=pl.Buffered(1))


def _dot(a, b):
    return jnp.dot(a, b, preferred_element_type=F32)


def _dot_nt(a, b):
    return lax.dot_general(a, b, (((1,), (1,)), ((), ())), preferred_element_type=F32)


def _dot_tn(a, b):
    return lax.dot_general(a, b, (((0,), (0,)), ((), ())), preferred_element_type=F32)


def _lane_tile(x, reps):
    return x if reps == 1 else jnp.concatenate([x] * reps, axis=1)


IN_CHUNK = 512


def _cast_weight_once(w_ref, wb_ref, chunk):
    @pl.when(pl.program_id(0) == 0)
    def _():
        for lo in range(0, w_ref.shape[1], chunk):
            wb_ref[:, lo:lo + chunk] = w_ref[:, lo:lo + chunk].astype(wb_ref.dtype)


def _in_proj_kernel(x_ref, w_ref, cos_ref, sin_ref, o_ref, wb_ref, *, n_out, rotary_cols):
    _cast_weight_once(w_ref, wb_ref, IN_CHUNK)
    x = x_ref[...].astype(BF16)
    for c in range(n_out // IN_CHUNK):
        lo = c * IN_CHUNK
        acc = _dot(x, wb_ref[:, lo:lo + IN_CHUNK])
        if lo < rotary_cols:
            cosv, sinv = cos_ref[...], sin_ref[...]
            for hh in range(IN_CHUNK // RET_DIM):
                blk = acc[:, hh * RET_DIM:(hh + 1) * RET_DIM]
                rot = blk * cosv + pltpu.roll(blk, RET_DIM // 2, axis=1) * sinv
                o_ref[:, lo + hh * RET_DIM:lo + (hh + 1) * RET_DIM] = rot.astype(o_ref.dtype)
        else:
            o_ref[:, lo:lo + IN_CHUNK] = acc.astype(o_ref.dtype)


def _in_proj(x_bf, w, layer, cos_t, sin_t, seq, rotary_cols, tm=WIDE_ROW_TILE):
    m, k = x_bf.shape
    n_out = w.shape[2]
    assert seq % tm == 0 and n_out % IN_CHUNK == 0
    s_tiles = seq // tm
    return pl.pallas_call(
        functools.partial(_in_proj_kernel, n_out=n_out, rotary_cols=rotary_cols),
        out_shape=jax.ShapeDtypeStruct((m, n_out), BF16),
        grid=(m // tm,),
        in_specs=[
            pl.BlockSpec((tm, k), lambda i: (i, 0)),
            _resident_layer(w, layer),
            pl.BlockSpec((tm, RET_DIM), lambda i: (i % s_tiles, 0)),
            pl.BlockSpec((tm, RET_DIM), lambda i: (i % s_tiles, 0)),
        ],
        out_specs=pl.BlockSpec((tm, n_out), lambda i: (i, 0)),
        scratch_shapes=[pltpu.VMEM((k, n_out), BF16)],
        compiler_params=_params(("arbitrary",)),
        name="in_proj",
    )(x_bf, w, cos_t, sin_t)


RET_CHUNK = 256


def _retention_tables():
    hh = np.arange(RET_HEADS, dtype=np.float32)
    log_g = np.log(np.float32(1.0) - np.float32(2.0) ** (np.float32(-5.0) - hh)).astype(np.float32)
    idx = np.arange(RET_CHUNK, dtype=np.float32)
    rel = idx[:, None] - idx[None, :]
    scale = np.float32(RET_DIM ** -0.5)
    decay = np.where(rel >= 0, np.exp(log_g[:, None, None] * np.maximum(rel, 0.0)), 0.0) * scale
    q_dec = np.exp(log_g[:, None] * (idx[None, :] + 1.0))
    k_dec = np.exp(log_g[:, None] * (RET_CHUNK - 1.0 - idx[None, :])) * scale
    chunk_g = np.exp(log_g * RET_CHUNK)
    bcast = lambda t: np.broadcast_to(t[:, :, None], (RET_HEADS, RET_CHUNK, LANES))
    return (decay.astype(np.float32), np.ascontiguousarray(bcast(q_dec), np.float32),
            np.ascontiguousarray(bcast(k_dec), np.float32), [float(g) for g in chunk_g])


def _retention_kernel(q_ref, k_ref, v_ref, g_ref, decay_ref, qdec_ref, kdec_ref, o_ref, state_ref,
                      *, chunk_g):
    @pl.when(pl.program_id(1) == 0)
    def _():
        state_ref[...] = jnp.zeros_like(state_ref)

    for hh in range(RET_HEADS):
        cols = slice(hh * RET_DIM, (hh + 1) * RET_DIM)
        q, k, v = q_ref[0, :, cols], k_ref[0, :, cols], v_ref[0, :, cols]
        state = state_ref[hh]
        s = _dot_nt(q, k) * decay_ref[hh]
        out = _dot(s.astype(BF16), v) + _dot(q, state.astype(BF16)) * qdec_ref[hh]
        k_scaled = (k.astype(F32) * kdec_ref[hh]).astype(BF16)
        state_ref[hh] = state * chunk_g[hh] + _dot_tn(k_scaled, v)
        mu = jnp.mean(out, axis=-1, keepdims=True)
        cen = out - mu
        var = jnp.mean(cen * cen, axis=-1, keepdims=True)
        gate = g_ref[0, :, cols].astype(F32)
        o_ref[0, :, cols] = (cen * lax.rsqrt(var + LN_EPS) * (gate * jax.nn.sigmoid(gate))).astype(o_ref.dtype)


def _retention(h3):
    bsz, seq, _ = h3.shape
    decay, q_dec, k_dec, chunk_g = _retention_tables()
    assert seq % RET_CHUNK == 0
    col = lambda j: pl.BlockSpec((1, RET_CHUNK, RET_W), lambda b, c, j=j: (b, c, j))
    return pl.pallas_call(
        functools.partial(_retention_kernel, chunk_g=chunk_g),
        out_shape=jax.ShapeDtypeStruct((bsz, seq, RET_W), BF16),
        grid=(bsz, seq // RET_CHUNK),
        in_specs=[col(0), col(1), col(2), col(3),
                  _resident((RET_HEADS, RET_CHUNK, RET_CHUNK)), _resident((RET_HEADS, RET_CHUNK, LANES)),
                  _resident((RET_HEADS, RET_CHUNK, LANES))],
        out_specs=pl.BlockSpec((1, RET_CHUNK, RET_W), lambda b, c: (b, c, 0)),
        scratch_shapes=[pltpu.VMEM((RET_HEADS, RET_DIM, RET_DIM), F32)],
        compiler_params=_params(("parallel", "arbitrary")),
        name="retention",
    )(h3, h3, h3, h3, jnp.asarray(decay), jnp.asarray(q_dec), jnp.asarray(k_dec))


SB_PAIR = LANES // SB_DIM
SB_KEYS = MXU_N
SB_DEAD = -104.0


def _split_bf16(x):
    hi = x.astype(BF16)
    lo = (x - hi.astype(F32)).astype(BF16)
    return hi, lo


def _stickbreak_kernel(q_ref, k_ref, v_ref, tri_ref, o_ref, acc_ref, run_ref):
    qi = pl.program_id(2)
    q = q_ref[0] * (SB_DIM ** -0.5)
    lane = lax.broadcasted_iota(jnp.int32, q.shape, 1)
    q_heads = [jnp.where((lane // SB_DIM) == p, q, jnp.zeros_like(q)) for p in range(SB_PAIR)]
    tri = tri_ref[...]

    r = lax.broadcasted_iota(jnp.int32, (SB_KEYS, SB_KEYS), 0)
    c = lax.broadcasted_iota(jnp.int32, (SB_KEYS, SB_KEYS), 1)
    past = c < r

    def add_blocks(blocks, runs, accs):
        kts, vts = [], []
        for jb, _, _ in blocks:
            start = pl.multiple_of(jb * SB_KEYS, SB_KEYS)
            kts.append(k_ref[0, pl.ds(start, SB_KEYS), :])
            vts.append(v_ref[0, pl.ds(start, SB_KEYS), :])
        chains = [(b, p) for b in range(len(blocks)) for p in range(SB_PAIR)]

        def masked(b, x):
            _, diagonal, valid = blocks[b]
            if diagonal:
                x = jnp.where(past, x, 0.0)
            if valid is not None:
                x = jnp.where(valid, x, 0.0)
            return x

        z = {bp: _dot_nt(q_heads[bp[1]], kts[bp[0]]) for bp in chains}
        log_take = {bp: jnp.minimum(z[bp], 0.0) - jnp.log(1.0 + jnp.exp(-jnp.abs(z[bp])))
                    for bp in chains}
        log_keep = {bp: masked(bp[0], log_take[bp] - z[bp]) for bp in chains}
        parts = {bp: _split_bf16(log_keep[bp]) for bp in chains}
        inner = {bp: _dot(parts[bp][0], tri) + _dot(parts[bp][1], tri) for bp in chains}
        runs, accs = list(runs), list(accs)
        weights = {}
        for b, p in chains:
            between = inner[b, p] + _lane_tile(runs[p], SB_KEYS // LANES)
            weights[b, p] = masked(b, jnp.exp(log_take[b, p] + between)).astype(BF16)
            runs[p] = runs[p] + jnp.sum(log_keep[b, p], axis=-1, keepdims=True)
        for b, p in chains:
            accs[p] = accs[p] + _dot(weights[b, p], vts[b])
        return runs, accs

    def any_alive(runs):
        return jnp.max(functools.reduce(jnp.maximum, runs)) > SB_DEAD

    zeros = [jnp.zeros((SB_KEYS, LANES), F32)] * SB_PAIR
    runs, accs = add_blocks([(qi, True, None), (jnp.maximum(qi - 1, 0), False, qi > 0)], zeros, zeros)
    for p in range(SB_PAIR):
        run_ref[p], acc_ref[p] = runs[p], accs[p]

    def more(carry):
        jb, _ = carry
        runs, accs = add_blocks([(jb, False, None)], [run_ref[p] for p in range(SB_PAIR)],
                                [acc_ref[p] for p in range(SB_PAIR)])
        for p in range(SB_PAIR):
            run_ref[p], acc_ref[p] = runs[p], accs[p]
        return jb - 1, any_alive(runs)

    lax.while_loop(lambda c: (c[0] >= 0) & c[1], more, (qi - 2, any_alive(runs)))
    out = acc_ref[0]
    for p in range(1, SB_PAIR):
        out = jnp.where((lane // SB_DIM) == p, acc_ref[p], out)
    o_ref[0] = out.astype(o_ref.dtype)


def _stickbreak(h3):
    bsz, seq, _ = h3.shape
    assert seq % SB_KEYS == 0
    n_pairs = SB_W // LANES
    base = 4 * RET_W // LANES
    idx = np.arange(SB_KEYS)
    tri = jnp.asarray((idx[:, None] > idx[None, :]).astype(np.float32), BF16)
    return pl.pallas_call(
        _stickbreak_kernel,
        out_shape=jax.ShapeDtypeStruct((bsz, seq, SB_W), BF16),
        grid=(bsz, n_pairs, seq // SB_KEYS),
        in_specs=[
            pl.BlockSpec((1, SB_KEYS, LANES), lambda b, p, i: (b, i, base + p)),
            pl.BlockSpec((1, seq, LANES), lambda b, p, i: (b, 0, base + n_pairs + p)),
            pl.BlockSpec((1, seq, LANES), lambda b, p, i: (b, 0, base + 2 * n_pairs + p)),
            _resident((SB_KEYS, SB_KEYS)),
        ],
        out_specs=pl.BlockSpec((1, SB_KEYS, LANES), lambda b, p, i: (b, i, p)),
        scratch_shapes=[pltpu.VMEM((SB_PAIR, SB_KEYS, LANES), F32),
                        pltpu.VMEM((SB_PAIR, SB_KEYS, LANES), F32)],
        compiler_params=_params(("parallel", "parallel", "arbitrary")),
        name="stickbreak",
    )(h3, h3, h3, tri)


DA_TQ = 512
DA_TK = 512
DA_QB = DA_TQ // BLOCK
DA_KB = DA_TK // BLOCK


def _bucket_tiles():
    rel = np.arange(2 * BLOCK)
    max_exact = REL_BUCKETS // 2
    nf = np.maximum(rel, 1).astype(np.float32)
    large = max_exact + (np.log(nf / np.float32(max_exact)) / np.float32(math.log(REL_MAX_DIST / max_exact))
                         * np.float32(REL_BUCKETS - max_exact)).astype(np.int32)
    bucket = np.where(rel < max_exact, rel, np.minimum(large, REL_BUCKETS - 1)).astype(np.int32)
    assert bucket[BLOCK - 1:].min() == REL_BUCKETS - 1
    idx = np.arange(BLOCK)
    d = idx[:, None] - idx[None, :]
    return np.stack([bucket[np.maximum(d, 0)], bucket[d + BLOCK]]).astype(np.int32)


def _diffattn_kernel(tab_ref, q_ref, k_ref, v_ref, bidx_ref, lam_ref, subg_ref, o_ref,
                     bias_ref, vext_ref, acc_ref, m_ref, s_ref, *, lam_init):
    head, b, qi = pl.program_id(0), pl.program_id(1), pl.program_id(2)
    far = REL_BUCKETS - 1

    @pl.when((b == 0) & (qi == 0))
    def _():
        row = lax.broadcasted_iota(jnp.int32, (BLOCK, BLOCK), 0)
        col = lax.broadcasted_iota(jnp.int32, (BLOCK, BLOCK), 1)
        bias_ref[0] = jnp.full((BLOCK, BLOCK), NEG_BIG, F32)
        for t in range(2):
            idx = bidx_ref[t]
            bias = jnp.zeros((BLOCK, BLOCK), F32)
            for bk in range(REL_BUCKETS):
                bias = jnp.where(idx == bk, tab_ref[bk, head] - tab_ref[far, head], bias)
            if t == 0:
                bias = jnp.where(col <= row, bias, NEG_BIG)
            bias_ref[1 + t] = bias
        bias_ref[3] = jnp.zeros((BLOCK, BLOCK), F32)

    @pl.when(qi == 0)
    def _():
        vext_ref[:, :LANES] = v_ref[0]
        vext_ref[:, LANES:] = jnp.ones((vext_ref.shape[0], LANES), vext_ref.dtype)

    q = q_ref[0] * (DIFF_DIM ** -0.5)
    lane = lax.broadcasted_iota(jnp.int32, q.shape, 1)
    q_maps = [jnp.where((lane // DIFF_DIM) == p, q, jnp.zeros_like(q)) for p in range(2)]
    acc_ref[...] = jnp.zeros_like(acc_ref)
    m_ref[...] = jnp.full(m_ref.shape, NEG_BIG, F32)

    def scores(j):
        start = pl.multiple_of(j * DA_TK, DA_TK)
        kt = k_ref[0, pl.ds(start, DA_TK), :]
        for p in range(2):
            s_ref[p] = _dot_nt(q_maps[p], kt)

    def absorb(j, near):
        start = pl.multiple_of(j * DA_TK, DA_TK)
        vt = vext_ref[pl.ds(start, DA_TK), :]
        if near:
            rows = []
            for rb in range(DA_QB):
                dist = (qi * DA_QB + rb) - j * DA_KB
                rows.append(jnp.concatenate(
                    [bias_ref[jnp.clip(dist - cb, -1, 2) + 1] for cb in range(DA_KB)], axis=1))
            bias = jnp.concatenate(rows, axis=0)
        maps = range(2)
        s = [s_ref[p] + bias if near else s_ref[p] for p in maps]
        m_old = [m_ref[p] for p in maps]
        m_new = [jnp.maximum(m_old[p], jnp.max(s[p], axis=-1, keepdims=True)) for p in maps]
        e = [jnp.exp(s[p] - _lane_tile(m_new[p], DA_KB)).astype(BF16) for p in maps]
        for p in maps:
            m_ref[p] = m_new[p]
            acc_ref[p] = acc_ref[p] * _lane_tile(jnp.exp(m_old[p] - m_new[p]), 2) + _dot(e[p], vt)

    def step(j, carry, near):
        absorb(j, near)
        scores(j + 1)
        return carry

    n_tiles = (qi * DA_TQ + DA_TQ - 1) // DA_TK + 1
    n_far = jnp.maximum((qi * DA_QB - 1) // DA_KB, 0)
    scores(0)
    lax.fori_loop(0, n_far, functools.partial(step, near=False), 0)
    lax.fori_loop(n_far, n_tiles - 1, functools.partial(step, near=True), 0)
    absorb(n_tiles - 1, True)

    lam_v = lam_ref[...]
    lam = (jnp.exp(jnp.sum(lam_v[0:1] * lam_v[1:2], axis=-1, keepdims=True))
           - jnp.exp(jnp.sum(lam_v[2:3] * lam_v[3:4], axis=-1, keepdims=True)) + lam_init)
    o0 = acc_ref[0, :, :LANES] / acc_ref[0, :, LANES:]
    o1 = acc_ref[1, :, :LANES] / acc_ref[1, :, LANES:]
    o = o0 - lam * o1
    o = o * lax.rsqrt(jnp.mean(o * o, axis=-1, keepdims=True) + LN_EPS) * subg_ref[...]
    o_ref[0] = (o * (1.0 - lam_init)).astype(o_ref.dtype)


def _diffattn(h3, rel_bias, lam_params, sub_g, lam_init):
    bsz, seq, _ = h3.shape
    assert seq % DA_TK == 0 and DA_TK % DA_TQ == 0
    return pl.pallas_call(
        functools.partial(_diffattn_kernel, lam_init=lam_init),
        out_shape=jax.ShapeDtypeStruct((bsz, seq, DIFF_W), BF16),
        grid=(DIFF_HEADS, bsz, seq // DA_TQ),
        in_specs=[
            pl.BlockSpec(memory_space=pltpu.SMEM),
            pl.BlockSpec((1, DA_TQ, LANES), lambda h, b, i: (b, i, h)),
            pl.BlockSpec((1, seq, LANES), lambda h, b, i: (b, 0, DIFF_HEADS + h)),
            pl.BlockSpec((1, seq, LANES), lambda h, b, i: (b, 0, 2 * DIFF_HEADS + h)),
            _resident((2, BLOCK, BLOCK)),
            _resident((4, DIFF_DIM)),
            _resident((1, 2 * DIFF_DIM)),
        ],
        out_specs=pl.BlockSpec((1, DA_TQ, LANES), lambda h, b, i: (b, i, h)),
        scratch_shapes=[pltpu.VMEM((4, BLOCK, BLOCK), F32),
                        pltpu.VMEM((seq, 2 * LANES), BF16),
                        pltpu.VMEM((2, DA_TQ, 2 * LANES), F32),
                        pltpu.VMEM((2, DA_TQ, LANES), F32),
                        pltpu.VMEM((2, DA_TQ, DA_TK), F32)],
        compiler_params=_params(("arbitrary", "arbitrary", "arbitrary")),
        name="diffattn",
    )(rel_bias, h3, h3, h3, jnp.asarray(_bucket_tiles()), lam_params, sub_g)


def _proj_ln_kernel(*refs, n_in):
    a_refs = refs[:n_in]
    w_ref, x_ref, g_ref, b_ref, o_ref, obf_ref, wb_ref = refs[n_in:]
    _cast_weight_once(w_ref, wb_ref, w_ref.shape[1])
    y = ALPHA * x_ref[...]
    row = 0
    for a_ref in a_refs:
        width = a_ref.shape[1]
        y = y + _dot(a_ref[...], wb_ref[row:row + width, :])
        row += width
    mu = jnp.mean(y, axis=-1, keepdims=True)
    cen = y - mu
    var = jnp.mean(cen * cen, axis=-1, keepdims=True)
    out = cen * lax.rsqrt(var + LN_EPS) * g_ref[...] + b_ref[...]
    o_ref[...] = out
    obf_ref[...] = out.astype(obf_ref.dtype)


def _proj_ln(acts, weight, layer, x, gain, bias, tm=ROW_TILE):
    m, d = x.shape
    assert m % tm == 0 and sum(a.shape[1] for a in acts) == weight.shape[1]
    n_in = len(acts)
    in_specs = [pl.BlockSpec((tm, a.shape[1]), lambda i: (i, 0)) for a in acts]
    in_specs += [_resident_layer(weight, layer)]
    in_specs += [pl.BlockSpec((tm, d), lambda i: (i, 0)), _resident((1, d)), _resident((1, d))]
    return pl.pallas_call(
        functools.partial(_proj_ln_kernel, n_in=n_in),
        out_shape=(jax.ShapeDtypeStruct((m, d), F32), jax.ShapeDtypeStruct((m, d), BF16)),
        grid=(m // tm,),
        in_specs=in_specs,
        out_specs=(pl.BlockSpec((tm, d), lambda i: (i, 0)), pl.BlockSpec((tm, d), lambda i: (i, 0))),
        scratch_shapes=[pltpu.VMEM(weight.shape[1:], BF16)],
        compiler_params=_params(("arbitrary",)),
        name="proj_ln",
    )(*acts, weight, x, gain.reshape(1, d), bias.reshape(1, d))


def _shift_rows(h, prev, n):
    row = lax.broadcasted_iota(jnp.int32, h.shape, 0)
    out = pltpu.roll(h, n, axis=0)
    for r in range(n):
        out = jnp.where(row == r, prev[SUBLANES - n + r:SUBLANES - n + r + 1], out)
    return out


def _ffn_up_kernel(x_ref, w_ref, cw_ref, cb_ref, o_ref, carry_ref, wb_ref, *, tiles_per_seq):
    _cast_weight_once(w_ref, wb_ref, 2 * MXU_N)
    x = x_ref[...]
    tm = x.shape[0]
    seq_start = (pl.program_id(0) % tiles_per_seq) == 0
    for c in range(D_FF // MXU_N):
        halves = []
        for base in (c * MXU_N, D_FF + c * MXU_N):
            cols = slice(base, base + MXU_N)
            h = _dot(x, wb_ref[:, cols])
            prev = jnp.where(seq_start, 0.0, carry_ref[:, cols])
            carry_ref[:, cols] = h[tm - SUBLANES:]
            cw = cw_ref[:, cols]
            halves.append(_shift_rows(h, prev, 2) * cw[0:1] + _shift_rows(h, prev, 1) * cw[1:2]
                          + h * cw[2:3] + cb_ref[:, cols])
        u, g = halves
        o_ref[:, c * MXU_N:(c + 1) * MXU_N] = (g * jax.nn.sigmoid(g) * u).astype(o_ref.dtype)


def _ffn_up(x_bf, w_up, layer, conv_w, conv_b, seq, tm=ROW_TILE):
    m, d = x_bf.shape
    assert seq % tm == 0
    return pl.pallas_call(
        functools.partial(_ffn_up_kernel, tiles_per_seq=seq // tm),
        out_shape=jax.ShapeDtypeStruct((m, D_FF), BF16),
        grid=(m // tm,),
        in_specs=[pl.BlockSpec((tm, d), lambda i: (i, 0)), _resident_layer(w_up, layer),
                  _resident((CONV_W, 2 * D_FF)), _resident((1, 2 * D_FF))],
        out_specs=pl.BlockSpec((tm, D_FF), lambda i: (i, 0)),
        scratch_shapes=[pltpu.VMEM((SUBLANES, 2 * D_FF), F32), pltpu.VMEM((d, 2 * D_FF), BF16)],
        compiler_params=_params(("arbitrary",)),
        name="ffn_up",
    )(x_bf, w_up, conv_w, conv_b.reshape(1, 2 * D_FF))


def _rotary_tables(seq):
    inv = ROPE_BASE ** (-jnp.arange(0, RET_DIM, 2, dtype=F32) / RET_DIM)
    ang = jnp.arange(seq, dtype=F32)[:, None] * inv[None, :]
    cos, sin = jnp.cos(ang), jnp.sin(ang)
    return jnp.concatenate([cos, cos], -1), jnp.concatenate([-sin, sin], -1)


def kernel(x, w_in_even, w_out_even, w_in_odd, w_out_odd, lam_q1, lam_k1, lam_q2, lam_k2, subln_g,
           rel_bias, w_up, conv_w, conv_b, w_down, ln1_g, ln1_b, ln2_g, ln2_b):
    bsz, seq, d = x.shape
    m = bsz * seq
    cos_t, sin_t = _rotary_tables(seq)
    xf = x.reshape(m, d)
    xb = xf
    for l in range(DEPTH):
        i = l // 2
        if l % 2 == 0:
            h = _in_proj(xb, w_in_even, i, cos_t, sin_t, seq, rotary_cols=2 * RET_W)
            h3 = h.reshape(bsz, seq, EVEN_IN)
            ret = _retention(h3).reshape(m, RET_W)
            sb = _stickbreak(h3).reshape(m, SB_W)
            xf, xb = _proj_ln([ret, sb], w_out_even, i, xf, ln1_g[l], ln1_b[l], tm=WIDE_ROW_TILE)
        else:
            h = _in_proj(xb, w_in_odd, i, cos_t, sin_t, seq, rotary_cols=0)
            lam_params = jnp.stack([lam_q1[i], lam_k1[i], lam_q2[i], lam_k2[i]]).astype(F32)
            lam_init = 0.8 - 0.6 * math.exp(-0.3 * l)
            o = _diffattn(h.reshape(bsz, seq, ODD_IN), rel_bias.astype(F32), lam_params,
                          subln_g[i].reshape(1, 2 * DIFF_DIM).astype(F32), lam_init)
            xf, xb = _proj_ln([o.reshape(m, DIFF_W)], w_out_odd, i, xf, ln1_g[l], ln1_b[l],
                              tm=WIDE_ROW_TILE)
        f = _ffn_up(xb, w_up, l, conv_w[l], conv_b[l], seq)
        xf, xb = _proj_ln([f], w_down, l, xf, ln2_g[l], ln2_b[l])
    return xf.reshape(bsz, seq, d)
```

```python
import functools
import math

import numpy as np
import jax
import jax.numpy as jnp
from jax import lax
from jax.experimental import pallas as pl
from jax.experimental.pallas import tpu as pltpu

D_MODEL = 1024
DEPTH = 4
BLOCK = 128
RET_HEADS = 4
RET_DIM = 128
SB_HEADS = 8
SB_DIM = 64
DIFF_HEADS = 8
DIFF_DIM = 64
REL_BUCKETS = 32
REL_MAX_DIST = 128
D_FF = 2816
CONV_W = 3
ALPHA = (2 * DEPTH) ** 0.25
LN_EPS = 1e-5
ROPE_BASE = 10000.0

RET_W = RET_HEADS * RET_DIM
SB_W = SB_HEADS * SB_DIM
EVEN_IN = 4 * RET_W + 3 * SB_W
DIFF_W = DIFF_HEADS * 2 * DIFF_DIM
ODD_IN = 3 * DIFF_W

LANES = 128
SUBLANES = 8
MXU_N = 256
VMEM_LIMIT = 56 * 1024 * 1024
NEG_BIG = -1e30
ROW_TILE = 512
WIDE_ROW_TILE = 1024

F32 = jnp.float32
BF16 = jnp.bfloat16


def _params(semantics):
    return pltpu.CompilerParams(dimension_semantics=semantics, vmem_limit_bytes=VMEM_LIMIT)


def _resident(shape):
    return pl.BlockSpec(shape, lambda *_: (0,) * len(shape), pipeline_mode=pl.Buffered(1))


def _resident_layer(w_stack, layer):
    return pl.BlockSpec((None,) + w_stack.shape[1:], lambda *_: (layer, 0, 0), pipeline_mode=pl.Buffered(1))


def _dot(a, b):
    return jnp.dot(a, b, preferred_element_type=F32)


def _dot_nt(a, b):
    return lax.dot_general(a, b, (((1,), (1,)), ((), ())), preferred_element_type=F32)


def _dot_tn(a, b):
    return lax.dot_general(a, b, (((0,), (0,)), ((), ())), preferred_element_type=F32)


def _lane_tile(x, reps):
    return x if reps == 1 else jnp.concatenate([x] * reps, axis=1)


IN_CHUNK = 512


def _cast_weight_once(w_ref, wb_ref, chunk):
    @pl.when(pl.program_id(0) == 0)
    def _():
        for lo in range(0, w_ref.shape[1], chunk):
            wb_ref[:, lo:lo + chunk] = w_ref[:, lo:lo + chunk].astype(wb_ref.dtype)


def _in_proj_kernel(x_ref, w_ref, cos_ref, sin_ref, o_ref, wb_ref, *, n_out, rotary_cols):
    _cast_weight_once(w_ref, wb_ref, IN_CHUNK)
    x = x_ref[...].astype(BF16)
    for c in range(n_out // IN_CHUNK):
        lo = c * IN_CHUNK
        acc = _dot(x, wb_ref[:, lo:lo + IN_CHUNK])
        if lo < rotary_cols:
            cosv, sinv = cos_ref[...], sin_ref[...]
            for hh in range(IN_CHUNK // RET_DIM):
                blk = acc[:, hh * RET_DIM:(hh + 1) * RET_DIM]
                rot = blk * cosv + pltpu.roll(blk, RET_DIM // 2, axis=1) * sinv
                o_ref[:, lo + hh * RET_DIM:lo + (hh + 1) * RET_DIM] = rot.astype(o_ref.dtype)
        else:
            o_ref[:, lo:lo + IN_CHUNK] = acc.astype(o_ref.dtype)


def _in_proj(x_bf, w, layer, cos_t, sin_t, seq, rotary_cols, tm=WIDE_ROW_TILE):
    m, k = x_bf.shape
    n_out = w.shape[2]
    assert seq % tm == 0 and n_out % IN_CHUNK == 0
    s_tiles = seq // tm
    return pl.pallas_call(
        functools.partial(_in_proj_kernel, n_out=n_out, rotary_cols=rotary_cols),
        out_shape=jax.ShapeDtypeStruct((m, n_out), BF16),
        grid=(m // tm,),
        in_specs=[
            pl.BlockSpec((tm, k), lambda i: (i, 0)),
            _resident_layer(w, layer),
            pl.BlockSpec((tm, RET_DIM), lambda i: (i % s_tiles, 0)),
            pl.BlockSpec((tm, RET_DIM), lambda i: (i % s_tiles, 0)),
        ],
        out_specs=pl.BlockSpec((tm, n_out), lambda i: (i, 0)),
        scratch_shapes=[pltpu.VMEM((k, n_out), BF16)],
        compiler_params=_params(("arbitrary",)),
        name="in_proj",
    )(x_bf, w, cos_t, sin_t)


RET_CHUNK = 256


def _retention_tables():
    hh = np.arange(RET_HEADS, dtype=np.float32)
    log_g = np.log(np.float32(1.0) - np.float32(2.0) ** (np.float32(-5.0) - hh)).astype(np.float32)
    idx = np.arange(RET_CHUNK, dtype=np.float32)
    rel = idx[:, None] - idx[None, :]
    scale = np.float32(RET_DIM ** -0.5)
    decay = np.where(rel >= 0, np.exp(log_g[:, None, None] * np.maximum(rel, 0.0)), 0.0) * scale
    q_dec = np.exp(log_g[:, None] * (idx[None, :] + 1.0))
    k_dec = np.exp(log_g[:, None] * (RET_CHUNK - 1.0 - idx[None, :])) * scale
    chunk_g = np.exp(log_g * RET_CHUNK)
    bcast = lambda t: np.broadcast_to(t[:, :, None], (RET_HEADS, RET_CHUNK, LANES))
    return (decay.astype(np.float32), np.ascontiguousarray(bcast(q_dec), np.float32),
            np.ascontiguousarray(bcast(k_dec), np.float32), [float(g) for g in chunk_g])


def _retention_kernel(q_ref, k_ref, v_ref, g_ref, decay_ref, qdec_ref, kdec_ref, o_ref, state_ref,
                      *, chunk_g):
    @pl.when(pl.program_id(1) == 0)
    def _():
        state_ref[...] = jnp.zeros_like(state_ref)

    for hh in range(RET_HEADS):
        cols = slice(hh * RET_DIM, (hh + 1) * RET_DIM)
        q, k, v = q_ref[0, :, cols], k_ref[0, :, cols], v_ref[0, :, cols]
        state = state_ref[hh]
        s = _dot_nt(q, k) * decay_ref[hh]
        out = _dot(s.astype(BF16), v) + _dot(q, state.astype(BF16)) * qdec_ref[hh]
        k_scaled = (k.astype(F32) * kdec_ref[hh]).astype(BF16)
        state_ref[hh] = state * chunk_g[hh] + _dot_tn(k_scaled, v)
        mu = jnp.mean(out, axis=-1, keepdims=True)
        cen = out - mu
        var = jnp.mean(cen * cen, axis=-1, keepdims=True)
        gate = g_ref[0, :, cols].astype(F32)
        o_ref[0, :, cols] = (cen * lax.rsqrt(var + LN_EPS) * (gate * jax.nn.sigmoid(gate))).astype(o_ref.dtype)


def _retention(h3):
    bsz, seq, _ = h3.shape
    decay, q_dec, k_dec, chunk_g = _retention_tables()
    assert seq % RET_CHUNK == 0
    col = lambda j: pl.BlockSpec((1, RET_CHUNK, RET_W), lambda b, c, j=j: (b, c, j))
    return pl.pallas_call(
        functools.partial(_retention_kernel, chunk_g=chunk_g),
        out_shape=jax.ShapeDtypeStruct((bsz, seq, RET_W), BF16),
        grid=(bsz, seq // RET_CHUNK),
        in_specs=[col(0), col(1), col(2), col(3),
                  _resident((RET_HEADS, RET_CHUNK, RET_CHUNK)), _resident((RET_HEADS, RET_CHUNK, LANES)),
                  _resident((RET_HEADS, RET_CHUNK, LANES))],
        out_specs=pl.BlockSpec((1, RET_CHUNK, RET_W), lambda b, c: (b, c, 0)),
        scratch_shapes=[pltpu.VMEM((RET_HEADS, RET_DIM, RET_DIM), F32)],
        compiler_params=_params(("parallel", "arbitrary")),
        name="retention",
    )(h3, h3, h3, h3, jnp.asarray(decay), jnp.asarray(q_dec), jnp.asarray(k_dec))


SB_PAIR = LANES // SB_DIM
SB_KEYS = MXU_N
SB_DEAD = -104.0


def _split_bf16(x):
    hi = x.astype(BF16)
    lo = (x - hi.astype(F32)).astype(BF16)
    return hi, lo


def _stickbreak_kernel(q_ref, k_ref, v_ref, tri_ref, o_ref, acc_ref, run_ref):
    qi = pl.program_id(2)
    q = q_ref[0] * (SB_DIM ** -0.5)
    lane = lax.broadcasted_iota(jnp.int32, q.shape, 1)
    q_heads = [jnp.where((lane // SB_DIM) == p, q, jnp.zeros_like(q)) for p in range(SB_PAIR)]
    tri = tri_ref[...]

    r = lax.broadcasted_iota(jnp.int32, (SB_KEYS, SB_KEYS), 0)
    c = lax.broadcasted_iota(jnp.int32, (SB_KEYS, SB_KEYS), 1)
    past = c < r

    def add_blocks(blocks, runs, accs):
        kts, vts = [], []
        for jb, _, _ in blocks:
            start = pl.multiple_of(jb * SB_KEYS, SB_KEYS)
            kts.append(k_ref[0, pl.ds(start, SB_KEYS), :])
            vts.append(v_ref[0, pl.ds(start, SB_KEYS), :])
        chains = [(b, p) for b in range(len(blocks)) for p in range(SB_PAIR)]

        def masked(b, x):
            _, diagonal, valid = blocks[b]
            if diagonal:
                x = jnp.where(past, x, 0.0)
            if valid is not None:
                x = jnp.where(valid, x, 0.0)
            return x

        z = {bp: _dot_nt(q_heads[bp[1]], kts[bp[0]]) for bp in chains}
        log_take = {bp: jnp.minimum(z[bp], 0.0) - jnp.log(1.0 + jnp.exp(-jnp.abs(z[bp])))
                    for bp in chains}
        log_keep = {bp: masked(bp[0], log_take[bp] - z[bp]) for bp in chains}
        parts = {bp: _split_bf16(log_keep[bp]) for bp in chains}
        inner = {bp: _dot(parts[bp][0], tri) + _dot(parts[bp][1], tri) for bp in chains}
        runs, accs = list(runs), list(accs)
        weights = {}
        for b, p in chains:
            between = inner[b, p] + _lane_tile(runs[p], SB_KEYS // LANES)
            weights[b, p] = masked(b, jnp.exp(log_take[b, p] + between)).astype(BF16)
            runs[p] = runs[p] + jnp.sum(log_keep[b, p], axis=-1, keepdims=True)
        for b, p in chains:
            accs[p] = accs[p] + _dot(weights[b, p], vts[b])
        return runs, accs

    def any_alive(runs):
        return jnp.max(functools.reduce(jnp.maximum, runs)) > SB_DEAD

    zeros = [jnp.zeros((SB_KEYS, LANES), F32)] * SB_PAIR
    runs, accs = add_blocks([(qi, True, None), (jnp.maximum(qi - 1, 0), False, qi > 0)], zeros, zeros)
    for p in range(SB_PAIR):
        run_ref[p], acc_ref[p] = runs[p], accs[p]

    def more(carry):
        jb, _ = carry
        runs, accs = add_blocks([(jb, False, None)], [run_ref[p] for p in range(SB_PAIR)],
                                [acc_ref[p] for p in range(SB_PAIR)])
        for p in range(SB_PAIR):
            run_ref[p], acc_ref[p] = runs[p], accs[p]
        return jb - 1, any_alive(runs)

    lax.while_loop(lambda c: (c[0] >= 0) & c[1], more, (qi - 2, any_alive(runs)))
    out = acc_ref[0]
    for p in range(1, SB_PAIR):
        out = jnp.where((lane // SB_DIM) == p, acc_ref[p], out)
    o_ref[0] = out.astype(o_ref.dtype)


def _stickbreak(h3):
    bsz, seq, _ = h3.shape
    assert seq % SB_KEYS == 0
    n_pairs = SB_W // LANES
    base = 4 * RET_W // LANES
    idx = np.arange(SB_KEYS)
    tri = jnp.asarray((idx[:, None] > idx[None, :]).astype(np.float32), BF16)
    return pl.pallas_call(
        _stickbreak_kernel,
        out_shape=jax.ShapeDtypeStruct((bsz, seq, SB_W), BF16),
        grid=(bsz, n_pairs, seq // SB_KEYS),
        in_specs=[
            pl.BlockSpec((1, SB_KEYS, LANES), lambda b, p, i: (b, i, base + p)),
            pl.BlockSpec((1, seq, LANES), lambda b, p, i: (b, 0, base + n_pairs + p)),
            pl.BlockSpec((1, seq, LANES), lambda b, p, i: (b, 0, base + 2 * n_pairs + p)),
            _resident((SB_KEYS, SB_KEYS)),
        ],
        out_specs=pl.BlockSpec((1, SB_KEYS, LANES), lambda b, p, i: (b, i, p)),
        scratch_shapes=[pltpu.VMEM((SB_PAIR, SB_KEYS, LANES), F32),
                        pltpu.VMEM((SB_PAIR, SB_KEYS, LANES), F32)],
        compiler_params=_params(("parallel", "parallel", "arbitrary")),
        name="stickbreak",
    )(h3, h3, h3, tri)


DA_TQ = 512
DA_TK = 512
DA_QB = DA_TQ // BLOCK
DA_KB = DA_TK // BLOCK


def _bucket_tiles():
    rel = np.arange(2 * BLOCK)
    max_exact = REL_BUCKETS // 2
    nf = np.maximum(rel, 1).astype(np.float32)
    large = max_exact + (np.log(nf / np.float32(max_exact)) / np.float32(math.log(REL_MAX_DIST / max_exact))
                         * np.float32(REL_BUCKETS - max_exact)).astype(np.int32)
    bucket = np.where(rel < max_exact, rel, np.minimum(large, REL_BUCKETS - 1)).astype(np.int32)
    assert bucket[BLOCK - 1:].min() == REL_BUCKETS - 1
    idx = np.arange(BLOCK)
    d = idx[:, None] - idx[None, :]
    return np.stack([bucket[np.maximum(d, 0)], bucket[d + BLOCK]]).astype(np.int32)


def _diffattn_kernel(tab_ref, q_ref, k_ref, v_ref, bidx_ref, lam_ref, subg_ref, o_ref,
                     bias_ref, vext_ref, acc_ref, m_ref, s_ref, *, lam_init):
    head, b, qi = pl.program_id(0), pl.program_id(1), pl.program_id(2)
    far = REL_BUCKETS - 1

    @pl.when((b == 0) & (qi == 0))
    def _():
        row = lax.broadcasted_iota(jnp.int32, (BLOCK, BLOCK), 0)
        col = lax.broadcasted_iota(jnp.int32, (BLOCK, BLOCK), 1)
        bias_ref[0] = jnp.full((BLOCK, BLOCK), NEG_BIG, F32)
        for t in range(2):
            idx = bidx_ref[t]
            bias = jnp.zeros((BLOCK, BLOCK), F32)
            for bk in range(REL_BUCKETS):
                bias = jnp.where(idx == bk, tab_ref[bk, head] - tab_ref[far, head], bias)
            if t == 0:
                bias = jnp.where(col <= row, bias, NEG_BIG)
            bias_ref[1 + t] = bias
        bias_ref[3] = jnp.zeros((BLOCK, BLOCK), F32)

    @pl.when(qi == 0)
    def _():
        vext_ref[:, :LANES] = v_ref[0]
        vext_ref[:, LANES:] = jnp.ones((vext_ref.shape[0], LANES), vext_ref.dtype)

    q = q_ref[0] * (DIFF_DIM ** -0.5)
    lane = lax.broadcasted_iota(jnp.int32, q.shape, 1)
    q_maps = [jnp.where((lane // DIFF_DIM) == p, q, jnp.zeros_like(q)) for p in range(2)]
    acc_ref[...] = jnp.zeros_like(acc_ref)
    m_ref[...] = jnp.full(m_ref.shape, NEG_BIG, F32)

    def scores(j):
        start = pl.multiple_of(j * DA_TK, DA_TK)
        kt = k_ref[0, pl.ds(start, DA_TK), :]
        for p in range(2):
            s_ref[p] = _dot_nt(q_maps[p], kt)

    def absorb(j, near):
        start = pl.multiple_of(j * DA_TK, DA_TK)
        vt = vext_ref[pl.ds(start, DA_TK), :]
        if near:
            rows = []
            for rb in range(DA_QB):
                dist = (qi * DA_QB + rb) - j * DA_KB
                rows.append(jnp.concatenate(
                    [bias_ref[jnp.clip(dist - cb, -1, 2) + 1] for cb in range(DA_KB)], axis=1))
            bias = jnp.concatenate(rows, axis=0)
        maps = range(2)
        s = [s_ref[p] + bias if near else s_ref[p] for p in maps]
        m_old = [m_ref[p] for p in maps]
        m_new = [jnp.maximum(m_old[p], jnp.max(s[p], axis=-1, keepdims=True)) for p in maps]
        e = [jnp.exp(s[p] - _lane_tile(m_new[p], DA_KB)).astype(BF16) for p in maps]
        for p in maps:
            m_ref[p] = m_new[p]
            acc_ref[p] = acc_ref[p] * _lane_tile(jnp.exp(m_old[p] - m_new[p]), 2) + _dot(e[p], vt)

    def step(j, carry, near):
        absorb(j, near)
        scores(j + 1)
        return carry

    n_tiles = (qi * DA_TQ + DA_TQ - 1) // DA_TK + 1
    n_far = jnp.maximum((qi * DA_QB - 1) // DA_KB, 0)
    scores(0)
    lax.fori_loop(0, n_far, functools.partial(step, near=False), 0)
    lax.fori_loop(n_far, n_tiles - 1, functools.partial(step, near=True), 0)
    absorb(n_tiles - 1, True)

    lam_v = lam_ref[...]
    lam = (jnp.exp(jnp.sum(lam_v[0:1] * lam_v[1:2], axis=-1, keepdims=True))
           - jnp.exp(jnp.sum(lam_v[2:3] * lam_v[3:4], axis=-1, keepdims=True)) + lam_init)
    o0 = acc_ref[0, :, :LANES] / acc_ref[0, :, LANES:]
    o1 = acc_ref[1, :, :LANES] / acc_ref[1, :, LANES:]
    o = o0 - lam * o1
    o = o * lax.rsqrt(jnp.mean(o * o, axis=-1, keepdims=True) + LN_EPS) * subg_ref[...]
    o_ref[0] = (o * (1.0 - lam_init)).astype(o_ref.dtype)


def _diffattn(h3, rel_bias, lam_params, sub_g, lam_init):
    bsz, seq, _ = h3.shape
    assert seq % DA_TK == 0 and DA_TK % DA_TQ == 0
    return pl.pallas_call(
        functools.partial(_diffattn_kernel, lam_init=lam_init),
        out_shape=jax.ShapeDtypeStruct((bsz, seq, DIFF_W), BF16),
        grid=(DIFF_HEADS, bsz, seq // DA_TQ),
        in_specs=[
            pl.BlockSpec(memory_space=pltpu.SMEM),
            pl.BlockSpec((1, DA_TQ, LANES), lambda h, b, i: (b, i, h)),
            pl.BlockSpec((1, seq, LANES), lambda h, b, i: (b, 0, DIFF_HEADS + h)),
            pl.BlockSpec((1, seq, LANES), lambda h, b, i: (b, 0, 2 * DIFF_HEADS + h)),
            _resident((2, BLOCK, BLOCK)),
            _resident((4, DIFF_DIM)),
            _resident((1, 2 * DIFF_DIM)),
        ],
        out_specs=pl.BlockSpec((1, DA_TQ, LANES), lambda h, b, i: (b, i, h)),
        scratch_shapes=[pltpu.VMEM((4, BLOCK, BLOCK), F32),
                        pltpu.VMEM((seq, 2 * LANES), BF16),
                        pltpu.VMEM((2, DA_TQ, 2 * LANES), F32),
                        pltpu.VMEM((2, DA_TQ, LANES), F32),
                        pltpu.VMEM((2, DA_TQ, DA_TK), F32)],
        compiler_params=_params(("arbitrary", "arbitrary", "arbitrary")),
        name="diffattn",
    )(rel_bias, h3, h3, h3, jnp.asarray(_bucket_tiles()), lam_params, sub_g)


def _proj_ln_kernel(*refs, n_in):
    a_refs = refs[:n_in]
    w_ref, x_ref, g_ref, b_ref, o_ref, obf_ref, wb_ref = refs[n_in:]
    _cast_weight_once(w_ref, wb_ref, w_ref.shape[1])
    y = ALPHA * x_ref[...]
    row = 0
    for a_ref in a_refs:
        width = a_ref.shape[1]
        y = y + _dot(a_ref[...], wb_ref[row:row + width, :])
        row += width
    mu = jnp.mean(y, axis=-1, keepdims=True)
    cen = y - mu
    var = jnp.mean(cen * cen, axis=-1, keepdims=True)
    out = cen * lax.rsqrt(var + LN_EPS) * g_ref[...] + b_ref[...]
    o_ref[...] = out
    obf_ref[...] = out.astype(obf_ref.dtype)


def _proj_ln(acts, weight, layer, x, gain, bias, tm=ROW_TILE):
    m, d = x.shape
    assert m % tm == 0 and sum(a.shape[1] for a in acts) == weight.shape[1]
    n_in = len(acts)
    in_specs = [pl.BlockSpec((tm, a.shape[1]), lambda i: (i, 0)) for a in acts]
    in_specs += [_resident_layer(weight, layer)]
    in_specs += [pl.BlockSpec((tm, d), lambda i: (i, 0)), _resident((1, d)), _resident((1, d))]
    return pl.pallas_call(
        functools.partial(_proj_ln_kernel, n_in=n_in),
        out_shape=(jax.ShapeDtypeStruct((m, d), F32), jax.ShapeDtypeStruct((m, d), BF16)),
        grid=(m // tm,),
        in_specs=in_specs,
        out_specs=(pl.BlockSpec((tm, d), lambda i: (i, 0)), pl.BlockSpec((tm, d), lambda i: (i, 0))),
        scratch_shapes=[pltpu.VMEM(weight.shape[1:], BF16)],
        compiler_params=_params(("arbitrary",)),
        name="proj_ln",
    )(*acts, weight, x, gain.reshape(1, d), bias.reshape(1, d))


def _shift_rows(h, prev, n):
    row = lax.broadcasted_iota(jnp.int32, h.shape, 0)
    out = pltpu.roll(h, n, axis=0)
    for r in range(n):
        out = jnp.where(row == r, prev[SUBLANES - n + r:SUBLANES - n + r + 1], out)
    return out


def _ffn_up_kernel(x_ref, w_ref, cw_ref, cb_ref, o_ref, carry_ref, wb_ref, *, tiles_per_seq):
    _cast_weight_once(w_ref, wb_ref, 2 * MXU_N)
    x = x_ref[...]
    tm = x.shape[0]
    seq_start = (pl.program_id(0) % tiles_per_seq) == 0
    for c in range(D_FF // MXU_N):
        halves = []
        for base in (c * MXU_N, D_FF + c * MXU_N):
            cols = slice(base, base + MXU_N)
            h = _dot(x, wb_ref[:, cols])
            prev = jnp.where(seq_start, 0.0, carry_ref[:, cols])
            carry_ref[:, cols] = h[tm - SUBLANES:]
            cw = cw_ref[:, cols]
            halves.append(_shift_rows(h, prev, 2) * cw[0:1] + _shift_rows(h, prev, 1) * cw[1:2]
                          + h * cw[2:3] + cb_ref[:, cols])
        u, g = halves
        o_ref[:, c * MXU_N:(c + 1) * MXU_N] = (g * jax.nn.sigmoid(g) * u).astype(o_ref.dtype)


def _ffn_up(x_bf, w_up, layer, conv_w, conv_b, seq, tm=ROW_TILE):
    m, d = x_bf.shape
    assert seq % tm == 0
    return pl.pallas_call(
        functools.partial(_ffn_up_kernel, tiles_per_seq=seq // tm),
        out_shape=jax.ShapeDtypeStruct((m, D_FF), BF16),
        grid=(m // tm,),
        in_specs=[pl.BlockSpec((tm, d), lambda i: (i, 0)), _resident_layer(w_up, layer),
                  _resident((CONV_W, 2 * D_FF)), _resident((1, 2 * D_FF))],
        out_specs=pl.BlockSpec((tm, D_FF), lambda i: (i, 0)),
        scratch_shapes=[pltpu.VMEM((SUBLANES, 2 * D_FF), F32), pltpu.VMEM((d, 2 * D_FF), BF16)],
        compiler_params=_params(("arbitrary",)),
        name="ffn_up",
    )(x_bf, w_up, conv_w, conv_b.reshape(1, 2 * D_FF))


def _rotary_tables(seq):
    inv = ROPE_BASE ** (-jnp.arange(0, RET_DIM, 2, dtype=F32) / RET_DIM)
    ang = jnp.arange(seq, dtype=F32)[:, None] * inv[None, :]
    cos, sin = jnp.cos(ang), jnp.sin(ang)
    return jnp.concatenate([cos, cos], -1), jnp.concatenate([-sin, sin], -1)


def kernel(x, w_in_even, w_out_even, w_in_odd, w_out_odd, lam_q1, lam_k1, lam_q2, lam_k2, subln_g,
           rel_bias, w_up, conv_w, conv_b, w_down, ln1_g, ln1_b, ln2_g, ln2_b):
    bsz, seq, d = x.shape
    m = bsz * seq
    cos_t, sin_t = _rotary_tables(seq)
    xf = x.reshape(m, d)
    xb = xf
    for l in range(DEPTH):
        i = l // 2
        if l % 2 == 0:
            h = _in_proj(xb, w_in_even, i, cos_t, sin_t, seq, rotary_cols=2 * RET_W)
            h3 = h.reshape(bsz, seq, EVEN_IN)
            ret = _retention(h3).reshape(m, RET_W)
            sb = _stickbreak(h3).reshape(m, SB_W)
            xf, xb = _proj_ln([ret, sb], w_out_even, i, xf, ln1_g[l], ln1_b[l], tm=WIDE_ROW_TILE)
        else:
            h = _in_proj(xb, w_in_odd, i, cos_t, sin_t, seq, rotary_cols=0)
            lam_params = jnp.stack([lam_q1[i], lam_k1[i], lam_q2[i], lam_k2[i]]).astype(F32)
            lam_init = 0.8 - 0.6 * math.exp(-0.3 * l)
            o = _diffattn(h.reshape(bsz, seq, ODD_IN), rel_bias.astype(F32), lam_params,
                          subln_g[i].reshape(1, 2 * DIFF_DIM).astype(F32), lam_init)
            xf, xb = _proj_ln([o.reshape(m, DIFF_W)], w_out_odd, i, xf, ln1_g[l], ln1_b[l],
                              tm=WIDE_ROW_TILE)
        f = _ffn_up(xb, w_up, l, conv_w[l], conv_b[l], seq)
        xf, xb = _proj_ln([f], w_down, l, xf, ln2_g[l], ln2_b[l], tm=WIDE_ROW_TILE)
    return xf.reshape(bsz, seq, d)
```

```python
import functools
import math

import numpy as np
import jax
import jax.numpy as jnp
from jax import lax
from jax.experimental import pallas as pl
from jax.experimental.pallas import tpu as pltpu

D_MODEL = 1024
DEPTH = 4
BLOCK = 128
RET_HEADS = 4
RET_DIM = 128
SB_HEADS = 8
SB_DIM = 64
DIFF_HEADS = 8
DIFF_DIM = 64
REL_BUCKETS = 32
REL_MAX_DIST = 128
D_FF = 2816
CONV_W = 3
ALPHA = (2 * DEPTH) ** 0.25
LN_EPS = 1e-5
ROPE_BASE = 10000.0

RET_W = RET_HEADS * RET_DIM
SB_W = SB_HEADS * SB_DIM
EVEN_IN = 4 * RET_W + 3 * SB_W
DIFF_W = DIFF_HEADS * 2 * DIFF_DIM
ODD_IN = 3 * DIFF_W

LANES = 128
SUBLANES = 8
MXU_N = 256
VMEM_LIMIT = 56 * 1024 * 1024
NEG_BIG = -1e30
ROW_TILE = 512
WIDE_ROW_TILE = 1024

F32 = jnp.float32
BF16 = jnp.bfloat16


def _params(semantics):
    return pltpu.CompilerParams(dimension_semantics=semantics, vmem_limit_bytes=VMEM_LIMIT)


def _resident(shape):
    return pl.BlockSpec(shape, lambda *_: (0,) * len(shape), pipeline_mode=pl.Buffered(1))


def _resident_layer(w_stack, layer):
    return pl.BlockSpec((None,) + w_stack.shape[1:], lambda *_: (layer, 0, 0), pipeline_mode=pl.Buffered(1))


def _dot(a, b):
    return jnp.dot(a, b, preferred_element_type=F32)


def _dot_nt(a, b):
    return lax.dot_general(a, b, (((1,), (1,)), ((), ())), preferred_element_type=F32)


def _dot_tn(a, b):
    return lax.dot_general(a, b, (((0,), (0,)), ((), ())), preferred_element_type=F32)


def _lane_tile(x, reps):
    return x if reps == 1 else jnp.concatenate([x] * reps, axis=1)


IN_CHUNK = 512


def _cast_weight_once(w_ref, wb_ref, chunk):
    @pl.when(pl.program_id(0) == 0)
    def _():
        for lo in range(0, w_ref.shape[1], chunk):
            wb_ref[:, lo:lo + chunk] = w_ref[:, lo:lo + chunk].astype(wb_ref.dtype)


def _in_proj_kernel(x_ref, w_ref, cos_ref, sin_ref, o_ref, wb_ref, *, n_out, rotary_cols):
    _cast_weight_once(w_ref, wb_ref, IN_CHUNK)
    x = x_ref[...].astype(BF16)
    for c in range(n_out // IN_CHUNK):
        lo = c * IN_CHUNK
        acc = _dot(x, wb_ref[:, lo:lo + IN_CHUNK])
        if lo < rotary_cols:
            cosv, sinv = cos_ref[...], sin_ref[...]
            for hh in range(IN_CHUNK // RET_DIM):
                blk = acc[:, hh * RET_DIM:(hh + 1) * RET_DIM]
                rot = blk * cosv + pltpu.roll(blk, RET_DIM // 2, axis=1) * sinv
                o_ref[:, lo + hh * RET_DIM:lo + (hh + 1) * RET_DIM] = rot.astype(o_ref.dtype)
        else:
            o_ref[:, lo:lo + IN_CHUNK] = acc.astype(o_ref.dtype)


def _in_proj(x_bf, w, layer, cos_t, sin_t, seq, rotary_cols, tm=WIDE_ROW_TILE):
    m, k = x_bf.shape
    n_out = w.shape[2]
    assert seq % tm == 0 and n_out % IN_CHUNK == 0
    s_tiles = seq // tm
    return pl.pallas_call(
        functools.partial(_in_proj_kernel, n_out=n_out, rotary_cols=rotary_cols),
        out_shape=jax.ShapeDtypeStruct((m, n_out), BF16),
        grid=(m // tm,),
        in_specs=[
            pl.BlockSpec((tm, k), lambda i: (i, 0)),
            _resident_layer(w, layer),
            pl.BlockSpec((tm, RET_DIM), lambda i: (i % s_tiles, 0)),
            pl.BlockSpec((tm, RET_DIM), lambda i: (i % s_tiles, 0)),
        ],
        out_specs=pl.BlockSpec((tm, n_out), lambda i: (i, 0)),
        scratch_shapes=[pltpu.VMEM((k, n_out), BF16)],
        compiler_params=_params(("arbitrary",)),
        name="in_proj",
    )(x_bf, w, cos_t, sin_t)


RET_CHUNK = 256


def _retention_tables():
    hh = np.arange(RET_HEADS, dtype=np.float32)
    log_g = np.log(np.float32(1.0) - np.float32(2.0) ** (np.float32(-5.0) - hh)).astype(np.float32)
    idx = np.arange(RET_CHUNK, dtype=np.float32)
    rel = idx[:, None] - idx[None, :]
    scale = np.float32(RET_DIM ** -0.5)
    decay = np.where(rel >= 0, np.exp(log_g[:, None, None] * np.maximum(rel, 0.0)), 0.0) * scale
    q_dec = np.exp(log_g[:, None] * (idx[None, :] + 1.0))
    k_dec = np.exp(log_g[:, None] * (RET_CHUNK - 1.0 - idx[None, :])) * scale
    chunk_g = np.exp(log_g * RET_CHUNK)
    bcast = lambda t: np.broadcast_to(t[:, :, None], (RET_HEADS, RET_CHUNK, LANES))
    return (decay.astype(np.float32), np.ascontiguousarray(bcast(q_dec), np.float32),
            np.ascontiguousarray(bcast(k_dec), np.float32), [float(g) for g in chunk_g])


def _retention_kernel(q_ref, k_ref, v_ref, g_ref, decay_ref, qdec_ref, kdec_ref, o_ref, state_ref,
                      *, chunk_g):
    @pl.when(pl.program_id(1) == 0)
    def _():
        state_ref[...] = jnp.zeros_like(state_ref)

    for hh in range(RET_HEADS):
        cols = slice(hh * RET_DIM, (hh + 1) * RET_DIM)
        q, k, v = q_ref[0, :, cols], k_ref[0, :, cols], v_ref[0, :, cols]
        state = state_ref[hh]
        s = _dot_nt(q, k) * decay_ref[hh]
        out = _dot(s.astype(BF16), v) + _dot(q, state.astype(BF16)) * qdec_ref[hh]
        k_scaled = (k.astype(F32) * kdec_ref[hh]).astype(BF16)
        state_ref[hh] = state * chunk_g[hh] + _dot_tn(k_scaled, v)
        mu = jnp.mean(out, axis=-1, keepdims=True)
        cen = out - mu
        var = jnp.mean(cen * cen, axis=-1, keepdims=True)
        gate = g_ref[0, :, cols].astype(F32)
        o_ref[0, :, cols] = (cen * lax.rsqrt(var + LN_EPS) * (gate * jax.nn.sigmoid(gate))).astype(o_ref.dtype)


def _retention(h3):
    bsz, seq, _ = h3.shape
    decay, q_dec, k_dec, chunk_g = _retention_tables()
    assert seq % RET_CHUNK == 0
    col = lambda j: pl.BlockSpec((1, RET_CHUNK, RET_W), lambda b, c, j=j: (b, c, j))
    return pl.pallas_call(
        functools.partial(_retention_kernel, chunk_g=chunk_g),
        out_shape=jax.ShapeDtypeStruct((bsz, seq, RET_W), BF16),
        grid=(bsz, seq // RET_CHUNK),
        in_specs=[col(0), col(1), col(2), col(3),
                  _resident((RET_HEADS, RET_CHUNK, RET_CHUNK)), _resident((RET_HEADS, RET_CHUNK, LANES)),
                  _resident((RET_HEADS, RET_CHUNK, LANES))],
        out_specs=pl.BlockSpec((1, RET_CHUNK, RET_W), lambda b, c: (b, c, 0)),
        scratch_shapes=[pltpu.VMEM((RET_HEADS, RET_DIM, RET_DIM), F32)],
        compiler_params=_params(("parallel", "arbitrary")),
        name="retention",
    )(h3, h3, h3, h3, jnp.asarray(decay), jnp.asarray(q_dec), jnp.asarray(k_dec))


SB_PAIR = LANES // SB_DIM
SB_KEYS = MXU_N
SB_DEAD = -104.0


def _split_bf16(x):
    hi = x.astype(BF16)
    lo = (x - hi.astype(F32)).astype(BF16)
    return hi, lo


def _stickbreak_kernel(q_ref, k_ref, v_ref, tri_ref, o_ref, acc_ref, run_ref):
    qi = pl.program_id(2)
    q = q_ref[0] * (SB_DIM ** -0.5)
    lane = lax.broadcasted_iota(jnp.int32, q.shape, 1)
    q_heads = [jnp.where((lane // SB_DIM) == p, q, jnp.zeros_like(q)) for p in range(SB_PAIR)]
    tri = tri_ref[...]

    r = lax.broadcasted_iota(jnp.int32, (SB_KEYS, SB_KEYS), 0)
    c = lax.broadcasted_iota(jnp.int32, (SB_KEYS, SB_KEYS), 1)
    past = c < r

    def add_blocks(blocks, runs, accs):
        kts, vts = [], []
        for jb, _, _ in blocks:
            start = pl.multiple_of(jb * SB_KEYS, SB_KEYS)
            kts.append(k_ref[0, pl.ds(start, SB_KEYS), :])
            vts.append(v_ref[0, pl.ds(start, SB_KEYS), :])
        chains = [(b, p) for b in range(len(blocks)) for p in range(SB_PAIR)]

        def masked(b, x):
            _, diagonal, valid = blocks[b]
            if diagonal:
                x = jnp.where(past, x, 0.0)
            if valid is not None:
                x = jnp.where(valid, x, 0.0)
            return x

        z = {bp: _dot_nt(q_heads[bp[1]], kts[bp[0]]) for bp in chains}
        log_take = {bp: jnp.minimum(z[bp], 0.0) - jnp.log(1.0 + jnp.exp(-jnp.abs(z[bp])))
                    for bp in chains}
        log_keep = {bp: masked(bp[0], log_take[bp] - z[bp]) for bp in chains}
        parts = {bp: _split_bf16(log_keep[bp]) for bp in chains}
        inner = {bp: _dot(parts[bp][0], tri) + _dot(parts[bp][1], tri) for bp in chains}
        runs, accs = list(runs), list(accs)
        weights = {}
        for b, p in chains:
            between = inner[b, p] + _lane_tile(runs[p], SB_KEYS // LANES)
            weights[b, p] = masked(b, jnp.exp(log_take[b, p] + between)).astype(BF16)
            runs[p] = runs[p] + jnp.sum(log_keep[b, p], axis=-1, keepdims=True)
        for b, p in chains:
            accs[p] = accs[p] + _dot(weights[b, p], vts[b])
        return runs, accs

    def any_alive(runs):
        return jnp.max(functools.reduce(jnp.maximum, runs)) > SB_DEAD

    zeros = [jnp.zeros((SB_KEYS, LANES), F32)] * SB_PAIR
    runs, accs = add_blocks([(qi, True, None), (jnp.maximum(qi - 1, 0), False, qi > 0)], zeros, zeros)
    for p in range(SB_PAIR):
        run_ref[p], acc_ref[p] = runs[p], accs[p]

    def more(carry):
        jb, _ = carry
        runs, accs = add_blocks([(jb, False, None)], [run_ref[p] for p in range(SB_PAIR)],
                                [acc_ref[p] for p in range(SB_PAIR)])
        for p in range(SB_PAIR):
            run_ref[p], acc_ref[p] = runs[p], accs[p]
        return jb - 1, any_alive(runs)

    lax.while_loop(lambda c: (c[0] >= 0) & c[1], more, (qi - 2, any_alive(runs)))
    out = acc_ref[0]
    for p in range(1, SB_PAIR):
        out = jnp.where((lane // SB_DIM) == p, acc_ref[p], out)
    o_ref[0] = out.astype(o_ref.dtype)


def _stickbreak(h3):
    bsz, seq, _ = h3.shape
    assert seq % SB_KEYS == 0
    n_pairs = SB_W // LANES
    base = 4 * RET_W // LANES
    idx = np.arange(SB_KEYS)
    tri = jnp.asarray((idx[:, None] > idx[None, :]).astype(np.float32), BF16)
    return pl.pallas_call(
        _stickbreak_kernel,
        out_shape=jax.ShapeDtypeStruct((bsz, seq, SB_W), BF16),
        grid=(bsz, n_pairs, seq // SB_KEYS),
        in_specs=[
            pl.BlockSpec((1, SB_KEYS, LANES), lambda b, p, i: (b, i, base + p)),
            pl.BlockSpec((1, seq, LANES), lambda b, p, i: (b, 0, base + n_pairs + p)),
            pl.BlockSpec((1, seq, LANES), lambda b, p, i: (b, 0, base + 2 * n_pairs + p)),
            _resident((SB_KEYS, SB_KEYS)),
        ],
        out_specs=pl.BlockSpec((1, SB_KEYS, LANES), lambda b, p, i: (b, i, p)),
        scratch_shapes=[pltpu.VMEM((SB_PAIR, SB_KEYS, LANES), F32),
                        pltpu.VMEM((SB_PAIR, SB_KEYS, LANES), F32)],
        compiler_params=_params(("parallel", "parallel", "arbitrary")),
        name="stickbreak",
    )(h3, h3, h3, tri)


DA_TQ = 512
DA_TK = 512
DA_QB = DA_TQ // BLOCK
DA_KB = DA_TK // BLOCK


def _bucket_tiles():
    rel = np.arange(2 * BLOCK)
    max_exact = REL_BUCKETS // 2
    nf = np.maximum(rel, 1).astype(np.float32)
    large = max_exact + (np.log(nf / np.float32(max_exact)) / np.float32(math.log(REL_MAX_DIST / max_exact))
                         * np.float32(REL_BUCKETS - max_exact)).astype(np.int32)
    bucket = np.where(rel < max_exact, rel, np.minimum(large, REL_BUCKETS - 1)).astype(np.int32)
    assert bucket[BLOCK - 1:].min() == REL_BUCKETS - 1
    idx = np.arange(BLOCK)
    d = idx[:, None] - idx[None, :]
    return np.stack([bucket[np.maximum(d, 0)], bucket[d + BLOCK]]).astype(np.int32)


def _diffattn_kernel(tab_ref, q_ref, k_ref, v_ref, bidx_ref, lam_ref, subg_ref, o_ref,
                     bias_ref, vext_ref, acc_ref, m_ref, s_ref, *, lam_init):
    head, b, qi = pl.program_id(0), pl.program_id(1), pl.program_id(2)
    far = REL_BUCKETS - 1

    @pl.when((b == 0) & (qi == 0))
    def _():
        row = lax.broadcasted_iota(jnp.int32, (BLOCK, BLOCK), 0)
        col = lax.broadcasted_iota(jnp.int32, (BLOCK, BLOCK), 1)
        bias_ref[0] = jnp.full((BLOCK, BLOCK), NEG_BIG, F32)
        for t in range(2):
            idx = bidx_ref[t]
            bias = jnp.zeros((BLOCK, BLOCK), F32)
            for bk in range(REL_BUCKETS):
                bias = jnp.where(idx == bk, tab_ref[bk, head] - tab_ref[far, head], bias)
            if t == 0:
                bias = jnp.where(col <= row, bias, NEG_BIG)
            bias_ref[1 + t] = bias
        bias_ref[3] = jnp.zeros((BLOCK, BLOCK), F32)

    @pl.when(qi == 0)
    def _():
        vext_ref[:, :LANES] = v_ref[0]
        vext_ref[:, LANES:] = jnp.ones((vext_ref.shape[0], LANES), vext_ref.dtype)

    q = q_ref[0] * (DIFF_DIM ** -0.5)
    lane = lax.broadcasted_iota(jnp.int32, q.shape, 1)
    q_maps = [jnp.where((lane // DIFF_DIM) == p, q, jnp.zeros_like(q)) for p in range(2)]
    acc_ref[...] = jnp.zeros_like(acc_ref)
    m_ref[...] = jnp.full(m_ref.shape, NEG_BIG, F32)

    def scores(j):
        start = pl.multiple_of(j * DA_TK, DA_TK)
        kt = k_ref[0, pl.ds(start, DA_TK), :]
        for p in range(2):
            s_ref[p] = _dot_nt(q_maps[p], kt)

    def absorb(j, near):
        start = pl.multiple_of(j * DA_TK, DA_TK)
        vt = vext_ref[pl.ds(start, DA_TK), :]
        if near:
            rows = []
            for rb in range(DA_QB):
                dist = (qi * DA_QB + rb) - j * DA_KB
                rows.append(jnp.concatenate(
                    [bias_ref[jnp.clip(dist - cb, -1, 2) + 1] for cb in range(DA_KB)], axis=1))
            bias = jnp.concatenate(rows, axis=0)
        maps = range(2)
        s = [s_ref[p] + bias if near else s_ref[p] for p in maps]
        m_old = [m_ref[p] for p in maps]
        m_new = [jnp.maximum(m_old[p], jnp.max(s[p], axis=-1, keepdims=True)) for p in maps]
        e = [jnp.exp(s[p] - _lane_tile(m_new[p], DA_KB)).astype(BF16) for p in maps]
        for p in maps:
            m_ref[p] = m_new[p]
            acc_ref[p] = acc_ref[p] * _lane_tile(jnp.exp(m_old[p] - m_new[p]), 2) + _dot(e[p], vt)

    def step(j, carry, near):
        absorb(j, near)
        scores(j + 1)
        return carry

    n_tiles = (qi * DA_TQ + DA_TQ - 1) // DA_TK + 1
    n_far = jnp.maximum((qi * DA_QB - 1) // DA_KB, 0)
    scores(0)
    lax.fori_loop(0, n_far, functools.partial(step, near=False), 0)
    lax.fori_loop(n_far, n_tiles - 1, functools.partial(step, near=True), 0)
    absorb(n_tiles - 1, True)

    lam_v = lam_ref[...]
    lam = (jnp.exp(jnp.sum(lam_v[0:1] * lam_v[1:2], axis=-1, keepdims=True))
           - jnp.exp(jnp.sum(lam_v[2:3] * lam_v[3:4], axis=-1, keepdims=True)) + lam_init)
    o0 = acc_ref[0, :, :LANES] / acc_ref[0, :, LANES:]
    o1 = acc_ref[1, :, :LANES] / acc_ref[1, :, LANES:]
    o = o0 - lam * o1
    o = o * lax.rsqrt(jnp.mean(o * o, axis=-1, keepdims=True) + LN_EPS) * subg_ref[...]
    o_ref[0] = (o * (1.0 - lam_init)).astype(o_ref.dtype)


def _diffattn(h3, rel_bias, lam_params, sub_g, lam_init):
    bsz, seq, _ = h3.shape
    assert seq % DA_TK == 0 and DA_TK % DA_TQ == 0
    return pl.pallas_call(
        functools.partial(_diffattn_kernel, lam_init=lam_init),
        out_shape=jax.ShapeDtypeStruct((bsz, seq, DIFF_W), BF16),
        grid=(DIFF_HEADS, bsz, seq // DA_TQ),
        in_specs=[
            pl.BlockSpec(memory_space=pltpu.SMEM),
            pl.BlockSpec((1, DA_TQ, LANES), lambda h, b, i: (b, i, h)),
            pl.BlockSpec((1, seq, LANES), lambda h, b, i: (b, 0, DIFF_HEADS + h)),
            pl.BlockSpec((1, seq, LANES), lambda h, b, i: (b, 0, 2 * DIFF_HEADS + h)),
            _resident((2, BLOCK, BLOCK)),
            _resident((4, DIFF_DIM)),
            _resident((1, 2 * DIFF_DIM)),
        ],
        out_specs=pl.BlockSpec((1, DA_TQ, LANES), lambda h, b, i: (b, i, h)),
        scratch_shapes=[pltpu.VMEM((4, BLOCK, BLOCK), F32),
                        pltpu.VMEM((seq, 2 * LANES), BF16),
                        pltpu.VMEM((2, DA_TQ, 2 * LANES), F32),
                        pltpu.VMEM((2, DA_TQ, LANES), F32),
                        pltpu.VMEM((2, DA_TQ, DA_TK), F32)],
        compiler_params=_params(("arbitrary", "arbitrary", "arbitrary")),
        name="diffattn",
    )(rel_bias, h3, h3, h3, jnp.asarray(_bucket_tiles()), lam_params, sub_g)


def _proj_ln_kernel(*refs, n_in):
    a_refs = refs[:n_in]
    w_ref, x_ref, g_ref, b_ref, o_ref = refs[n_in:n_in + 5]
    copy_refs, wb_ref = refs[n_in + 5:-1], refs[-1]
    _cast_weight_once(w_ref, wb_ref, w_ref.shape[1])
    y = ALPHA * x_ref[...]
    row = 0
    for a_ref in a_refs:
        width = a_ref.shape[1]
        y = y + _dot(a_ref[...], wb_ref[row:row + width, :])
        row += width
    mu = jnp.mean(y, axis=-1, keepdims=True)
    cen = y - mu
    var = jnp.mean(cen * cen, axis=-1, keepdims=True)
    out = cen * lax.rsqrt(var + LN_EPS) * g_ref[...] + b_ref[...]
    o_ref[...] = out
    for obf_ref in copy_refs:
        obf_ref[...] = out.astype(obf_ref.dtype)


def _proj_ln(acts, weight, layer, x, gain, bias, tm=WIDE_ROW_TILE, bf16_copy=True):
    m, d = x.shape
    tile = pl.BlockSpec((tm, d), lambda i: (i, 0))
    n_out = 2 if bf16_copy else 1
    assert m % tm == 0 and sum(a.shape[1] for a in acts) == weight.shape[1]
    n_in = len(acts)
    in_specs = [pl.BlockSpec((tm, a.shape[1]), lambda i: (i, 0)) for a in acts]
    in_specs += [_resident_layer(weight, layer)]
    in_specs += [pl.BlockSpec((tm, d), lambda i: (i, 0)), _resident((1, d)), _resident((1, d))]
    return pl.pallas_call(
        functools.partial(_proj_ln_kernel, n_in=n_in),
        out_shape=(jax.ShapeDtypeStruct((m, d), F32), jax.ShapeDtypeStruct((m, d), BF16))[:n_out],
        grid=(m // tm,),
        in_specs=in_specs,
        out_specs=(tile, tile)[:n_out],
        scratch_shapes=[pltpu.VMEM(weight.shape[1:], BF16)],
        compiler_params=_params(("arbitrary",)),
        name="proj_ln",
    )(*acts, weight, x, gain.reshape(1, d), bias.reshape(1, d))


def _shift_rows(h, prev, n):
    row = lax.broadcasted_iota(jnp.int32, h.shape, 0)
    out = pltpu.roll(h, n, axis=0)
    for r in range(n):
        out = jnp.where(row == r, prev[SUBLANES - n + r:SUBLANES - n + r + 1], out)
    return out


def _ffn_up_kernel(x_ref, w_ref, cw_ref, cb_ref, o_ref, carry_ref, wb_ref, *, tiles_per_seq):
    _cast_weight_once(w_ref, wb_ref, 2 * MXU_N)
    x = x_ref[...]
    tm = x.shape[0]
    seq_start = (pl.program_id(0) % tiles_per_seq) == 0
    for c in range(D_FF // MXU_N):
        halves = []
        for base in (c * MXU_N, D_FF + c * MXU_N):
            cols = slice(base, base + MXU_N)
            h = _dot(x, wb_ref[:, cols])
            prev = jnp.where(seq_start, 0.0, carry_ref[:, cols])
            carry_ref[:, cols] = h[tm - SUBLANES:]
            cw = cw_ref[:, cols]
            halves.append(_shift_rows(h, prev, 2) * cw[0:1] + _shift_rows(h, prev, 1) * cw[1:2]
                          + h * cw[2:3] + cb_ref[:, cols])
        u, g = halves
        o_ref[:, c * MXU_N:(c + 1) * MXU_N] = (g * jax.nn.sigmoid(g) * u).astype(o_ref.dtype)


def _ffn_up(x_bf, w_up, layer, conv_w, conv_b, seq, tm=ROW_TILE):
    m, d = x_bf.shape
    assert seq % tm == 0
    return pl.pallas_call(
        functools.partial(_ffn_up_kernel, tiles_per_seq=seq // tm),
        out_shape=jax.ShapeDtypeStruct((m, D_FF), BF16),
        grid=(m // tm,),
        in_specs=[pl.BlockSpec((tm, d), lambda i: (i, 0)), _resident_layer(w_up, layer),
                  _resident((CONV_W, 2 * D_FF)), _resident((1, 2 * D_FF))],
        out_specs=pl.BlockSpec((tm, D_FF), lambda i: (i, 0)),
        scratch_shapes=[pltpu.VMEM((SUBLANES, 2 * D_FF), F32), pltpu.VMEM((d, 2 * D_FF), BF16)],
        compiler_params=_params(("arbitrary",)),
        name="ffn_up",
    )(x_bf, w_up, conv_w, conv_b.reshape(1, 2 * D_FF))


def _rotary_tables(seq):
    inv = ROPE_BASE ** (-jnp.arange(0, RET_DIM, 2, dtype=F32) / RET_DIM)
    ang = jnp.arange(seq, dtype=F32)[:, None] * inv[None, :]
    cos, sin = jnp.cos(ang), jnp.sin(ang)
    return jnp.concatenate([cos, cos], -1), jnp.concatenate([-sin, sin], -1)


def kernel(x, w_in_even, w_out_even, w_in_odd, w_out_odd, lam_q1, lam_k1, lam_q2, lam_k2, subln_g,
           rel_bias, w_up, conv_w, conv_b, w_down, ln1_g, ln1_b, ln2_g, ln2_b):
    bsz, seq, d = x.shape
    m = bsz * seq
    cos_t, sin_t = _rotary_tables(seq)
    xf = x.reshape(m, d)
    xb = xf
    for l in range(DEPTH):
        i = l // 2
        if l % 2 == 0:
            h = _in_proj(xb, w_in_even, i, cos_t, sin_t, seq, rotary_cols=2 * RET_W)
            h3 = h.reshape(bsz, seq, EVEN_IN)
            ret = _retention(h3).reshape(m, RET_W)
            sb = _stickbreak(h3).reshape(m, SB_W)
            xf, xb = _proj_ln([ret, sb], w_out_even, i, xf, ln1_g[l], ln1_b[l])
        else:
            h = _in_proj(xb, w_in_odd, i, cos_t, sin_t, seq, rotary_cols=0)
            lam_params = jnp.stack([lam_q1[i], lam_k1[i], lam_q2[i], lam_k2[i]]).astype(F32)
            lam_init = 0.8 - 0.6 * math.exp(-0.3 * l)
            o = _diffattn(h.reshape(bsz, seq, ODD_IN), rel_bias.astype(F32), lam_params,
                          subln_g[i].reshape(1, 2 * DIFF_DIM).astype(F32), lam_init)
            xf, xb = _proj_ln([o.reshape(m, DIFF_W)], w_out_odd, i, xf, ln1_g[l], ln1_b[l])
        f = _ffn_up(xb, w_up, l, conv_w[l], conv_b[l], seq)
        if l + 1 < DEPTH:
            xf, xb = _proj_ln([f], w_down, l, xf, ln2_g[l], ln2_b[l])
        else:
            xf, = _proj_ln([f], w_down, l, xf, ln2_g[l], ln2_b[l], bf16_copy=False)
    return xf.reshape(bsz, seq, d)
```
